```python
import jax
import jax.numpy as jnp
from jax import lax
import numpy as np

D_MODEL = 1024
BATCH = 4
SEQ = 8192
DEPTH = 2

GRID_W = 64
CTX_LEN = 256
HEAD_DIM = 64
A_Q_HEADS = 6
A_KV_HEADS = 2
A_GROUP = A_Q_HEADS // A_KV_HEADS
A_WINDOW = 128
A_BLOCK = 128
B_WIDTH = 256
B_CONV = 3
C_HEADS = 6
NA_ROWS = 8
NA_COLS = 16
A_WIDTH = A_Q_HEADS * HEAD_DIM
A_KV_WIDTH = A_KV_HEADS * HEAD_DIM
C_WIDTH = C_HEADS * HEAD_DIM
MIX_WIDTH = A_WIDTH + B_WIDTH + C_WIDTH
IN_SPLITS = (A_WIDTH, A_KV_WIDTH, A_KV_WIDTH, B_WIDTH, B_WIDTH, B_WIDTH, C_WIDTH, C_WIDTH, C_WIDTH)
IN_WIDTH = sum(IN_SPLITS)
N_EXPERTS = 16
EXPERT_FF = 512
CAPACITY_FACTOR = 2
ROPE_THETA = 10000.0
ROPE_AXIS_DIM = HEAD_DIM // 2
LN_EPS = 1e-6
N_MOD = 6
DEEPNORM_ALPHA = (2 * DEPTH) ** 0.25
DEEPNORM_BETA = (8 * DEPTH) ** -0.25
NEG_INF = -1e30

kernel_name = "hybrid_parallel_group_dit_block"


def _split_last(a, sizes):
    out, start = [], 0
    for n in sizes:
        out.append(a[..., start:start + n])
        start += n
    return out


def _heads(a, n_heads):
    return a.reshape(a.shape[0], a.shape[1], n_heads, HEAD_DIM)


def _ln_stats(x):
    xf = x.astype(jnp.float32)
    mu = jnp.mean(xf, axis=-1, keepdims=True)
    var = jnp.mean(jnp.square(xf - mu), axis=-1, keepdims=True)
    return (xf - mu) * lax.rsqrt(var + LN_EPS)


def ln_plain(x):
    return _ln_stats(x).astype(x.dtype)


def ln_affine(x, g, b):
    return (_ln_stats(x) * g.astype(jnp.float32) + b.astype(jnp.float32)).astype(x.dtype)


def modulate(h, shift, scale):
    return h * (1 + scale) + shift


def joint_softmax(parts):
    probs = jax.nn.softmax(jnp.concatenate([p.astype(jnp.float32) for p in parts], axis=-1), axis=-1)
    return _split_last(probs, [p.shape[-1] for p in parts])


def axial_rope(x, row_pos, col_pos):
    inv = ROPE_THETA ** (-jnp.arange(0, ROPE_AXIS_DIM, 2, dtype=jnp.float32) / ROPE_AXIS_DIM)

    def rot(xa, pos):
        ang = pos.astype(jnp.float32)[:, None] * inv[None, :]
        cos = jnp.cos(ang)[None, :, None, :].astype(x.dtype)
        sin = jnp.sin(ang)[None, :, None, :].astype(x.dtype)
        x1, x2 = jnp.split(xa, 2, axis=-1)
        return jnp.concatenate([x1 * cos - x2 * sin, x2 * cos + x1 * sin], axis=-1)

    return jnp.concatenate([rot(x[..., :ROPE_AXIS_DIM], row_pos), rot(x[..., ROPE_AXIS_DIM:], col_pos)], axis=-1)


def windowed_gqa_latent(q, k, v, k_ctx, v_ctx, sink):
    bsz, seq = q.shape[:2]
    nb = seq // A_BLOCK
    scale = HEAD_DIM ** -0.5
    qb = q.reshape(bsz, nb, A_BLOCK, A_KV_HEADS, A_GROUP, HEAD_DIM)
    pad = ((0, 0), (A_BLOCK, A_BLOCK), (0, 0), (0, 0))
    kp = jnp.pad(k, pad).reshape(bsz, nb + 2, A_BLOCK, A_KV_HEADS, HEAD_DIM)
    vp = jnp.pad(v, pad).reshape(bsz, nb + 2, A_BLOCK, A_KV_HEADS, HEAD_DIM)
    kw = jnp.concatenate([kp[:, :-2], kp[:, 1:-1], kp[:, 2:]], axis=2)
    vw = jnp.concatenate([vp[:, :-2], vp[:, 1:-1], vp[:, 2:]], axis=2)
    s_win = jnp.einsum('bnqkgd,bnjkd->bnkgqj', qb, kw).astype(jnp.float32) * scale
    blk = jnp.arange(nb)[:, None] * A_BLOCK
    qpos = blk + jnp.arange(A_BLOCK)[None, :]
    kpos = blk - A_BLOCK + jnp.arange(3 * A_BLOCK)[None, :]
    valid = ((jnp.abs(qpos[:, :, None] - kpos[:, None, :]) <= A_WINDOW)
             & (kpos[:, None, :] >= 0) & (kpos[:, None, :] < seq))
    s_win = jnp.where(valid[None, :, None, None], s_win, NEG_INF)
    s_ctx = jnp.einsum('bnqkgd,bjkd->bnkgqj', qb, k_ctx).astype(jnp.float32) * scale
    s_sink = jnp.broadcast_to(sink.reshape(A_KV_HEADS, A_GROUP)[None, None, :, :, None, None].astype(jnp.float32),
                              s_win.shape[:-1] + (1,))
    p_win, p_ctx, _ = joint_softmax([s_win, s_ctx, s_sink])
    o = (jnp.einsum('bnkgqj,bnjkd->bnqkgd', p_win.astype(v.dtype), vw)
         + jnp.einsum('bnkgqj,bjkd->bnqkgd', p_ctx.astype(v.dtype), v_ctx))
    return o.reshape(bsz, seq, A_WIDTH)


def context_gqa(q, k, v, sink):
    bsz, ln = q.shape[:2]
    qg = q.reshape(bsz, ln, A_KV_HEADS, A_GROUP, HEAD_DIM)
    s = jnp.einsum('blkgd,bjkd->bkglj', qg, k).astype(jnp.float32) * HEAD_DIM ** -0.5
    s_sink = jnp.broadcast_to(sink.reshape(A_KV_HEADS, A_GROUP)[None, :, :, None, None].astype(jnp.float32),
                              s.shape[:-1] + (1,))
    p, _ = joint_softmax([s, s_sink])
    o = jnp.einsum('bkglj,bjkd->blkgd', p.astype(v.dtype), v)
    return o.reshape(bsz, ln, A_WIDTH)


def gated_short_conv(xin, gate_b, gate_c, w):
    u = gate_c * xin
    tlen = u.shape[1]
    up = jnp.pad(u, ((0, 0), (1, 1), (0, 0)))
    y = up[:, :tlen] * w[0] + up[:, 1:tlen + 1] * w[1] + up[:, 2:] * w[2]
    return gate_b * y


def neighbourhood_attention_latent(q, k, v, k_ctx, v_ctx, rpb):
    bsz, seq = q.shape[:2]
    rows = seq // GRID_W
    kh = min(NA_ROWS, rows)
    scale = HEAD_DIM ** -0.5
    qg = q.reshape(bsz, rows, GRID_W, C_HEADS, HEAD_DIM)
    kg = k.reshape(bsz, rows, GRID_W, C_HEADS, HEAD_DIM)
    vg = v.reshape(bsz, rows, GRID_W, C_HEADS, HEAD_DIM)
    r = jnp.arange(rows)
    row_start = jnp.clip(r - kh // 2, 0, rows - kh)
    key_rows = row_start[:, None] + jnp.arange(kh)[None, :]
    kn = kg[:, key_rows]
    vn = vg[:, key_rows]
    s_nb = jnp.einsum('brqhd,brachd->brhqac', qg, kn).astype(jnp.float32) * scale
    cq = jnp.arange(GRID_W)
    col_start = jnp.clip(cq - NA_COLS // 2, 0, GRID_W - NA_COLS)
    col_ok = (cq[None, :] >= col_start[:, None]) & (cq[None, :] < col_start[:, None] + NA_COLS)
    roff = key_rows - r[:, None] + (NA_ROWS - 1)
    coff = jnp.clip(cq[None, :] - cq[:, None], -(NA_COLS - 1), NA_COLS - 1) + (NA_COLS - 1)
    bias = rpb[:, roff[:, None, :, None], coff[None, :, None, :]]
    s_nb = s_nb + jnp.transpose(bias, (1, 0, 2, 3, 4))[None].astype(jnp.float32)
    s_nb = jnp.where(col_ok[None, None, None, :, None, :], s_nb, NEG_INF)
    s_nb = s_nb.reshape(bsz, rows, C_HEADS, GRID_W, kh * GRID_W)
    s_ctx = jnp.einsum('brqhd,bjhd->brhqj', qg, k_ctx).astype(jnp.float32) * scale
    p_nb, p_ctx = joint_softmax([s_nb, s_ctx])
    p_nb = p_nb.reshape(bsz, rows, C_HEADS, GRID_W, kh, GRID_W).astype(v.dtype)
    o = (jnp.einsum('brhqac,brachd->brqhd', p_nb, vn)
         + jnp.einsum('brhqj,bjhd->brqhd', p_ctx.astype(v.dtype), v_ctx))
    return o.reshape(bsz, seq, C_WIDTH)


def context_mha(q, k, v):
    bsz, ln = q.shape[:2]
    s = jnp.einsum('blhd,bjhd->bhlj', q, k).astype(jnp.float32) * HEAD_DIM ** -0.5
    p = jax.nn.softmax(s, axis=-1).astype(v.dtype)
    return jnp.einsum('bhlj,bjhd->blhd', p, v).reshape(bsz, ln, C_WIDTH)


def expert_choice_ffn(h, w_router, w_gate, w_up, w_down):
    bsz, tlen, dm = h.shape
    cap = CAPACITY_FACTOR * tlen // N_EXPERTS
    aff = jax.nn.softmax(jnp.einsum('btd,de->bte', h, w_router).astype(jnp.float32), axis=-1)
    gate, idx = lax.top_k(jnp.transpose(aff, (0, 2, 1)), cap)
    xe = jax.vmap(lambda hb, ib: hb[ib])(h, idx)
    a = jnp.einsum('becd,edf->becf', xe, w_gate)
    u = jnp.einsum('becd,edf->becf', xe, w_up)
    y = jnp.einsum('becf,efd->becd', jax.nn.silu(a) * u, w_down) * gate[..., None].astype(h.dtype)
    return jax.vmap(lambda yb, ib: jnp.zeros((tlen, dm), yb.dtype).at[ib.reshape(-1)].add(yb.reshape(-1, dm)))(y, idx)


def setup_inputs(seed: int = 0) -> dict:
    key = jax.random.key(seed)
    ks = jax.random.split(key, 20)
    f32 = jnp.float32
    nrm = lambda k, shape, s: jax.random.normal(k, shape, f32) * s
    return {
        "x": nrm(ks[0], (BATCH, SEQ, D_MODEL), 1.0),
        "c": nrm(ks[1], (BATCH, D_MODEL), 1.0),
        "ctx": nrm(ks[2], (BATCH, CTX_LEN, D_MODEL), 1.0),
        "c_ctx": nrm(ks[3], (D_MODEL,), 1.0),
        "w_mod": nrm(ks[4], (DEPTH, D_MODEL, N_MOD * D_MODEL), D_MODEL ** -0.5),
        "b_mod": nrm(ks[5], (DEPTH, N_MOD * D_MODEL), 0.02),
        "w_in": nrm(ks[6], (DEPTH, D_MODEL, IN_WIDTH), D_MODEL ** -0.5),
        "conv_w": nrm(ks[7], (DEPTH, B_CONV, B_WIDTH), B_CONV ** -0.5),
        "attn_sink": nrm(ks[8], (DEPTH, A_Q_HEADS), 0.5),
        "na_rpb": nrm(ks[9], (DEPTH, C_HEADS, 2 * NA_ROWS - 1, 2 * NA_COLS - 1), 0.1),
        "w_out": nrm(ks[10], (DEPTH, MIX_WIDTH, D_MODEL), MIX_WIDTH ** -0.5 * DEEPNORM_BETA),
        "ln1_g": 1.0 + nrm(ks[11], (DEPTH, D_MODEL), 0.02),
        "ln1_b": nrm(ks[12], (DEPTH, D_MODEL), 0.02),
        "w_router": nrm(ks[13], (DEPTH, D_MODEL, N_EXPERTS), D_MODEL ** -0.5),
        "w_gate": nrm(ks[14], (DEPTH, N_EXPERTS, D_MODEL, EXPERT_FF), D_MODEL ** -0.5),
        "w_up": nrm(ks[15], (DEPTH, N_EXPERTS, D_MODEL, EXPERT_FF), D_MODEL ** -0.5),
        "w_down": nrm(ks[16], (DEPTH, N_EXPERTS, EXPERT_FF, D_MODEL), EXPERT_FF ** -0.5 * DEEPNORM_BETA),
        "ln2_g": 1.0 + nrm(ks[17], (DEPTH, D_MODEL), 0.02),
        "ln2_b": nrm(ks[18], (DEPTH, D_MODEL), 0.02),
    }


def reference(x, c, ctx, c_ctx, w_mod, b_mod, w_in, conv_w, attn_sink, na_rpb, w_out,
              ln1_g, ln1_b, w_router, w_gate, w_up, w_down, ln2_g, ln2_b):
    seq = x.shape[1]
    t = jnp.arange(seq)
    row_pos = t // GRID_W
    col_pos = t % GRID_W
    for l in range(DEPTH):
        last = l == DEPTH - 1
        mod = jax.nn.silu(c) @ w_mod[l] + b_mod[l]
        mod_c = jax.nn.silu(c_ctx) @ w_mod[l] + b_mod[l]
        sh1, sc1, g1, sh2, sc2, g2 = jnp.split(mod[:, None, :], N_MOD, axis=-1)
        csh1, csc1, cg1, csh2, csc2, cg2 = jnp.split(mod_c, N_MOD, axis=-1)

        h = modulate(ln_plain(x), sh1, sc1)
        hc = modulate(ln_plain(ctx), csh1, csc1)
        qa, ka, va, bx, bb, bc, qn, kn, vn = _split_last(h @ w_in[l], IN_SPLITS)
        qa_c, ka_c, va_c, bx_c, bb_c, bc_c, qn_c, kn_c, vn_c = _split_last(hc @ w_in[l], IN_SPLITS)
        ka_c = _heads(ka_c, A_KV_HEADS)
        va_c = _heads(va_c, A_KV_HEADS)
        kn_c = _heads(kn_c, C_HEADS)
        vn_c = _heads(vn_c, C_HEADS)

        o_a = windowed_gqa_latent(axial_rope(_heads(qa, A_Q_HEADS), row_pos, col_pos),
                                  axial_rope(_heads(ka, A_KV_HEADS), row_pos, col_pos),
                                  _heads(va, A_KV_HEADS), ka_c, va_c, attn_sink[l])
        o_b = gated_short_conv(bx, bb, bc, conv_w[l])
        o_c = neighbourhood_attention_latent(_heads(qn, C_HEADS), _heads(kn, C_HEADS), _heads(vn, C_HEADS),
                                             kn_c, vn_c, na_rpb[l])
        mix = jnp.concatenate([o_a, o_b, o_c], axis=-1) @ w_out[l]
        x_mid = ln_affine(DEEPNORM_ALPHA * x + g1 * mix, ln1_g[l], ln1_b[l])

        if not last:
            o_a_c = context_gqa(_heads(qa_c, A_Q_HEADS), ka_c, va_c, attn_sink[l])
            o_b_c = gated_short_conv(bx_c, bb_c, bc_c, conv_w[l])
            o_c_c = context_mha(_heads(qn_c, C_HEADS), kn_c, vn_c)
            mix_c = jnp.concatenate([o_a_c, o_b_c, o_c_c], axis=-1) @ w_out[l]
            ctx_mid = ln_affine(DEEPNORM_ALPHA * ctx + cg1 * mix_c, ln1_g[l], ln1_b[l])
            h2c = modulate(ln_plain(ctx_mid), csh2, csc2)
            ffn_c = expert_choice_ffn(h2c, w_router[l], w_gate[l], w_up[l], w_down[l])
            ctx = ln_affine(DEEPNORM_ALPHA * ctx_mid + cg2 * ffn_c, ln2_g[l], ln2_b[l])

        h2 = modulate(ln_plain(x_mid), sh2, sc2)
        ffn = expert_choice_ffn(h2, w_router[l], w_gate[l], w_up[l], w_down[l])
        x = ln_affine(DEEPNORM_ALPHA * x_mid + g2 * ffn, ln2_g[l], ln2_b[l])
    return x
```

```python
import functools

import numpy as np
import jax
import jax.numpy as jnp
from jax import lax
from jax.experimental import pallas as pl
from jax.experimental.pallas import tpu as pltpu

HEAD_DIM = 64
GRID_W = 64
A_Q_HEADS = 6
A_KV_HEADS = 2
A_GROUP = A_Q_HEADS // A_KV_HEADS
A_WINDOW = 128
A_BLOCK = 128
B_WIDTH = 256
C_HEADS = 6
NA_ROWS = 8
NA_COLS = 16
A_WIDTH = A_Q_HEADS * HEAD_DIM
A_KV_WIDTH = A_KV_HEADS * HEAD_DIM
C_WIDTH = C_HEADS * HEAD_DIM
N_EXPERTS = 16
CAPACITY_FACTOR = 2
ROPE_THETA = 10000.0
ROPE_AXIS_DIM = HEAD_DIM // 2
LN_EPS = 1e-6
N_MOD = 6
NEG_INF = -1e30

LANES = 128
MOD_ROWS = 8
HALO_ROWS = 16
WIN_BLOCKS = 5
N_BIAS_PATTERNS = 5
VMEM_LIMIT = 56 * 1024 * 1024

BF16 = jnp.bfloat16
F32 = jnp.float32


def _cparams(vmem=None):
    return pltpu.CompilerParams(vmem_limit_bytes=vmem) if vmem else None


def _dot(a, b):
    return jnp.dot(a, b, preferred_element_type=F32)


def _dot_nt(a, b):
    return lax.dot_general(a, b, (((1,), (1,)), ((), ())), preferred_element_type=F32)


def _ln(x):
    mu = jnp.mean(x, axis=-1, keepdims=True)
    xc = x - mu
    var = jnp.mean(xc * xc, axis=-1, keepdims=True)
    return xc * lax.rsqrt(var + LN_EPS)


def _mod_body(c_ref, w_ref, b_ref, o_ref):
    c = c_ref[...]
    h = (c * jax.nn.sigmoid(c)).astype(BF16)
    o_ref[...] = _dot(h, w_ref[...].astype(BF16)) + b_ref[...]


def _modulation(cond, w_mod, b_mod):
    depth, d, n = w_mod.shape
    tn = n // 4
    return pl.pallas_call(
        _mod_body,
        grid=(depth, n // tn),
        in_specs=[
            pl.BlockSpec((MOD_ROWS, d), lambda l, j: (0, 0)),
            pl.BlockSpec((None, d, tn), lambda l, j: (l, 0, j)),
            pl.BlockSpec((None, 1, tn), lambda l, j: (l, 0, j)),
        ],
        out_specs=pl.BlockSpec((None, MOD_ROWS, tn), lambda l, j: (l, 0, j)),
        out_shape=jax.ShapeDtypeStruct((depth, MOD_ROWS, n), F32),
        compiler_params=_cparams(VMEM_LIMIT),
        name="modulation",
    )(cond, w_mod, b_mod.reshape(depth, 1, n))


QK_W = A_WIDTH + HEAD_DIM * A_KV_HEADS
O_VA = QK_W
O_B = O_VA + A_KV_WIDTH
O_QN = O_B + 3 * B_WIDTH
O_KVN = O_QN + C_WIDTH
IN_WIDTH = O_KVN + 2 * C_WIDTH


def _inproj_body(x_ref, sh_ref, sc_ref, cos_ref, sin_ref, w_ref,
                 qa_ref, kva_ref, ub_ref, qn_ref, kvn_ref):
    h = _ln(x_ref[...]) * (1.0 + sc_ref[...]) + sh_ref[...]
    hb = h.astype(BF16)
    scale = HEAD_DIM ** -0.5
    qk = _dot(hb, w_ref[:, 0:QK_W])
    cos = cos_ref[...]
    sin = sin_ref[...]
    lane = lax.broadcasted_iota(jnp.int32, cos.shape, 1)
    first = (lane % ROPE_AXIS_DIM) < (ROPE_AXIS_DIM // 2)
    half = ROPE_AXIS_DIM // 2
    parts = []
    for j in range(QK_W // LANES):
        t = qk[:, j * LANES:(j + 1) * LANES]
        partner = jnp.where(first, pltpu.roll(t, LANES - half, 1), pltpu.roll(t, half, 1))
        parts.append(t * cos + partner * sin)
    qa_ref[...] = (jnp.concatenate(parts[:-1], axis=1) * scale).astype(BF16)
    va = _dot(hb, w_ref[:, O_VA:O_B])
    kva_ref[...] = jnp.concatenate([parts[-1], va], axis=1).astype(BF16)
    g = _dot(hb, w_ref[:, O_B:O_QN])
    bx, bb, bc = g[:, :B_WIDTH], g[:, B_WIDTH:2 * B_WIDTH], g[:, 2 * B_WIDTH:]
    ub_ref[...] = jnp.concatenate([bc * bx, bb], axis=1).astype(BF16)
    qn_ref[...] = (_dot(hb, w_ref[:, O_QN:O_KVN]) * scale).astype(BF16)
    kvn_ref[...] = _dot(hb, w_ref[:, O_KVN:IN_WIDTH]).astype(BF16)


def _inproj(x, sh, sc, cos, sin, w, tm):
    bsz, seq, d = x.shape
    tok = lambda width: pl.BlockSpec((None, tm, width), lambda b, i: (b, i, 0))
    vec = pl.BlockSpec((None, 1, d), lambda b, i: (b, 0, 0))
    tab = pl.BlockSpec((tm, LANES), lambda b, i: (i, 0))
    widths = (A_WIDTH, 2 * A_KV_WIDTH, 2 * B_WIDTH, C_WIDTH, 2 * C_WIDTH)
    return pl.pallas_call(
        _inproj_body,
        grid=(bsz, seq // tm),
        in_specs=[tok(d), vec, vec, tab, tab, pl.BlockSpec((d, IN_WIDTH), lambda b, i: (0, 0))],
        out_specs=[tok(wd) for wd in widths],
        out_shape=[jax.ShapeDtypeStruct((bsz, seq, wd), BF16) for wd in widths],
        compiler_params=_cparams(VMEM_LIMIT),
        name="inproj",
    )(x, sh, sc, cos, sin, w)


def _split_heads(q, lo):
    zero = jnp.zeros_like(q)
    return jnp.concatenate([jnp.where(lo, q, zero), jnp.where(lo, zero, q)], axis=0)


def _gqa(qa, k_cat, v_cat, sink_ref, mask, m_rows):
    lo = lax.broadcasted_iota(jnp.int32, (m_rows, LANES), 1) < HEAD_DIM
    tiles = A_WIDTH // LANES
    qs = jnp.concatenate([_split_heads(qa[:, j * LANES:(j + 1) * LANES], lo) for j in range(tiles)], axis=0)
    s = _dot_nt(qs, k_cat)
    es, dens = [], []
    for blk in range(2 * tiles):
        head = (blk % 2) * A_GROUP + blk // 2
        sb = s[blk * m_rows:(blk + 1) * m_rows]
        if mask is not None:
            sb = jnp.where(mask, sb, NEG_INF)
        sk = sink_ref[head]
        m = jnp.maximum(jnp.max(sb, axis=1, keepdims=True), sk)
        e = jnp.exp(sb - m)
        dens.append(jnp.sum(e, axis=1, keepdims=True) + jnp.exp(sk - m))
        es.append(e.astype(BF16))
    o = _dot(jnp.concatenate(es, axis=0), v_cat)
    outs = []
    for j in range(tiles):
        o_lo = o[(2 * j) * m_rows:(2 * j + 1) * m_rows] / dens[2 * j]
        o_hi = o[(2 * j + 1) * m_rows:(2 * j + 2) * m_rows] / dens[2 * j + 1]
        outs.append(jnp.where(lo, o_lo, o_hi))
    return jnp.concatenate(outs, axis=1)


def _mha_pair(qt, k_cat, v_cat, bias_pair, n_biased, m_rows):
    lo = lax.broadcasted_iota(jnp.int32, (m_rows, LANES), 1) < HEAD_DIM
    s = _dot_nt(_split_heads(qt, lo), k_cat)
    if bias_pair is not None:
        s_w = s[:, :n_biased] + bias_pair
        s_c = s[:, n_biased:]
        m = jnp.maximum(jnp.max(s_w, axis=1, keepdims=True), jnp.max(s_c, axis=1, keepdims=True))
        e_w = jnp.exp(s_w - m)
        e_c = jnp.exp(s_c - m)
        den = jnp.sum(e_w, axis=1, keepdims=True) + jnp.sum(e_c, axis=1, keepdims=True)
        e = jnp.concatenate([e_w.astype(BF16), e_c.astype(BF16)], axis=1)
    else:
        m = jnp.max(s, axis=1, keepdims=True)
        e32 = jnp.exp(s - m)
        den = jnp.sum(e32, axis=1, keepdims=True)
        e = e32.astype(BF16)
    o = _dot(e, v_cat) / den
    return jnp.where(lo, o[:m_rows], o[m_rows:])


def _short_conv(ub, prev_row, next_row, w_ref, m_rows):
    u = ub[:, :B_WIDTH].astype(F32)
    bb = ub[:, B_WIDTH:].astype(F32)
    row = lax.broadcasted_iota(jnp.int32, u.shape, 0)
    u_m1 = jnp.where(row == 0, prev_row, pltpu.roll(u, 1, 0))
    u_p1 = jnp.where(row == m_rows - 1, next_row, pltpu.roll(u, m_rows - 1, 0))
    y = u_m1 * w_ref[0:1, :] + u * w_ref[1:2, :] + u_p1 * w_ref[2:3, :]
    return bb * y


def _mix_latent_body(sink_ref, qa_ref, kp_ref, ko_ref, kn_ref, ubp_ref, ubo_ref, ubn_ref, qn_ref,
                     w0_ref, w1_ref, w2_ref, w3_ref, w4_ref, kvac_ref, kvnc_ref, bias_ref, cw_ref,
                     o_ref, *, seq):
    n = pl.program_id(1)
    nb = pl.num_programs(1)
    m_rows = A_BLOCK
    k_cat = jnp.concatenate([kp_ref[:, :A_KV_WIDTH], ko_ref[:, :A_KV_WIDTH], kn_ref[:, :A_KV_WIDTH],
                             kvac_ref[:, :A_KV_WIDTH]], axis=0)
    v_cat = jnp.concatenate([kp_ref[:, A_KV_WIDTH:], ko_ref[:, A_KV_WIDTH:], kn_ref[:, A_KV_WIDTH:],
                             kvac_ref[:, A_KV_WIDTH:]], axis=0)
    n_keys = k_cat.shape[0]
    qpos = n * A_BLOCK + lax.broadcasted_iota(jnp.int32, (m_rows, n_keys), 0)
    col = lax.broadcasted_iota(jnp.int32, (m_rows, n_keys), 1)
    kpos = (n - 1) * A_BLOCK + col
    mask = (col >= 3 * A_BLOCK) | ((jnp.abs(qpos - kpos) <= A_WINDOW) & (kpos >= 0) & (kpos < seq))
    o_a = _gqa(qa_ref[...], k_cat, v_cat, sink_ref, mask, m_rows)
    prev_row = jnp.where(n > 0, ubp_ref[HALO_ROWS - 1:HALO_ROWS, :B_WIDTH].astype(F32), 0.0)
    next_row = jnp.where(n < nb - 1, ubn_ref[0:1, :B_WIDTH].astype(F32), 0.0)
    o_b = _short_conv(ubo_ref[...], prev_row, next_row, cw_ref, m_rows)
    wins = (w0_ref, w1_ref, w2_ref, w3_ref, w4_ref)
    n_win = WIN_BLOCKS * A_BLOCK
    outs = []
    for j in range(C_WIDTH // LANES):
        ks = slice(j * LANES, (j + 1) * LANES)
        vs = slice(C_WIDTH + j * LANES, C_WIDTH + (j + 1) * LANES)
        kc = jnp.concatenate([w[:, ks] for w in wins] + [kvnc_ref[:, ks]], axis=0)
        vc = jnp.concatenate([w[:, vs] for w in wins] + [kvnc_ref[:, vs]], axis=0)
        bias_pair = jnp.concatenate([bias_ref[2 * j], bias_ref[2 * j + 1]], axis=0)
        outs.append(_mha_pair(qn_ref[:, ks], kc, vc, bias_pair, n_win, m_rows))
    o_c = jnp.concatenate(outs, axis=1)
    o_ref[...] = jnp.concatenate([o_a, o_b, o_c], axis=1).astype(BF16)


def _mix_latent(sink, qa, kva, ub, qn, kvn, kva_c, kvn_c, bias, conv_w):
    bsz, seq, _ = qa.shape
    nb = seq // A_BLOCK
    lctx = kva_c.shape[1]
    blk = A_BLOCK
    hb = blk // HALO_ROWS
    n_halo = seq // HALO_ROWS
    tok = lambda width, fn: pl.BlockSpec((None, blk, width), fn)
    own = lambda b, n: (b, n, 0)
    win = lambda i: (lambda b, n: (b, jnp.clip(n - 2, 0, nb - WIN_BLOCKS) + i, 0))
    pattern = lambda b, n: (jnp.minimum(n, 2) + jnp.maximum(n - (nb - 3), 0), 0, 0, 0)
    in_specs = [
        pl.BlockSpec(memory_space=pltpu.SMEM),
        tok(A_WIDTH, own),
        tok(2 * A_KV_WIDTH, lambda b, n: (b, jnp.maximum(n - 1, 0), 0)),
        tok(2 * A_KV_WIDTH, own),
        tok(2 * A_KV_WIDTH, lambda b, n: (b, jnp.minimum(n + 1, nb - 1), 0)),
        pl.BlockSpec((None, HALO_ROWS, 2 * B_WIDTH), lambda b, n: (b, jnp.maximum(n * hb - 1, 0), 0)),
        tok(2 * B_WIDTH, own),
        pl.BlockSpec((None, HALO_ROWS, 2 * B_WIDTH), lambda b, n: (b, jnp.minimum((n + 1) * hb, n_halo - 1), 0)),
        tok(C_WIDTH, own),
    ] + [tok(2 * C_WIDTH, win(i)) for i in range(WIN_BLOCKS)] + [
        pl.BlockSpec((None, lctx, 2 * A_KV_WIDTH), lambda b, n: (b, 0, 0)),
        pl.BlockSpec((None, lctx, 2 * C_WIDTH), lambda b, n: (b, 0, 0)),
        pl.BlockSpec((None, C_HEADS, blk, WIN_BLOCKS * blk), pattern),
        pl.BlockSpec(conv_w.shape, lambda b, n: (0, 0)),
    ]
    return pl.pallas_call(
        functools.partial(_mix_latent_body, seq=seq),
        grid=(bsz, nb),
        in_specs=in_specs,
        out_specs=tok(A_WIDTH + B_WIDTH + C_WIDTH, own),
        out_shape=jax.ShapeDtypeStruct((bsz, seq, A_WIDTH + B_WIDTH + C_WIDTH), BF16),
        compiler_params=_cparams(VMEM_LIMIT),
        name="mix_latent",
    )(sink, qa, kva, kva, kva, ub, ub, ub, qn, kvn, kvn, kvn, kvn, kvn, kva_c, kvn_c, bias, conv_w)


def _mix_ctx_body(sink_ref, qa_ref, kva_ref, ub_ref, qn_ref, kvn_ref, cw_ref, o_ref):
    m_rows = qa_ref.shape[0]
    o_a = _gqa(qa_ref[...], kva_ref[:, :A_KV_WIDTH], kva_ref[:, A_KV_WIDTH:], sink_ref, None, m_rows)
    o_b = _short_conv(ub_ref[...], 0.0, 0.0, cw_ref, m_rows)
    outs = []
    for j in range(C_WIDTH // LANES):
        ks = slice(j * LANES, (j + 1) * LANES)
        vs = slice(C_WIDTH + j * LANES, C_WIDTH + (j + 1) * LANES)
        outs.append(_mha_pair(qn_ref[:, ks], kvn_ref[:, ks], kvn_ref[:, vs], None, 0, m_rows))
    o_ref[...] = jnp.concatenate([o_a, o_b] + outs, axis=1).astype(BF16)


def _mix_ctx(sink, qa, kva, ub, qn, kvn, conv_w):
    bsz, lctx, _ = qa.shape
    full = lambda width: pl.BlockSpec((None, lctx, width), lambda b: (b, 0, 0))
    width = A_WIDTH + B_WIDTH + C_WIDTH
    return pl.pallas_call(
        _mix_ctx_body,
        grid=(bsz,),
        in_specs=[pl.BlockSpec(memory_space=pltpu.SMEM), full(A_WIDTH), full(2 * A_KV_WIDTH),
                  full(2 * B_WIDTH), full(C_WIDTH), full(2 * C_WIDTH),
                  pl.BlockSpec(conv_w.shape, lambda b: (0, 0))],
        out_specs=full(width),
        out_shape=jax.ShapeDtypeStruct((bsz, lctx, width), BF16),
        compiler_params=_cparams(VMEM_LIMIT),
        name="mix_ctx",
    )(sink, qa, kva, ub, qn, kvn, conv_w)


def _na_bias_tables(rpb, nb):
    per_blk = A_BLOCK // GRID_W
    rows = nb * per_blk
    kh = min(NA_ROWS, rows)
    qc = np.arange(GRID_W)[:, None]
    kc = np.arange(GRID_W)[None, :]
    coff = np.clip(kc - qc, -(NA_COLS - 1), NA_COLS - 1) + (NA_COLS - 1)
    cs = np.clip(qc - NA_COLS // 2, 0, GRID_W - NA_COLS)
    col_ok = (kc >= cs) & (kc < cs + NA_COLS)
    onehot = (coff[None] == np.arange(2 * NA_COLS - 1)[:, None, None]).astype(np.float32)
    tiles = jnp.einsum('hrc,cqk->hrqk', rpb, onehot, precision=lax.Precision.HIGHEST)
    tiles = jnp.where(col_ok, tiles, NEG_INF)
    masked = jnp.full((rpb.shape[0], GRID_W, GRID_W), NEG_INF, F32)
    patterns = []
    for blk in (0, 1, 2, nb - 2, nb - 1):
        wb = min(max(blk - 2, 0), nb - WIN_BLOCKS)
        q_rows = []
        for qi in range(per_blk):
            qr = per_blk * blk + qi
            rs = min(max(qr - kh // 2, 0), rows - kh)
            k_tiles = []
            for kj in range(WIN_BLOCKS * per_blk):
                kr = per_blk * wb + kj
                k_tiles.append(tiles[:, kr - qr + NA_ROWS - 1] if rs <= kr < rs + kh else masked)
            q_rows.append(jnp.concatenate(k_tiles, axis=-1))
        patterns.append(jnp.concatenate(q_rows, axis=-2))
    return jnp.stack(patterns)


def _router_affinity(h2b, wr_ref):
    logits = _dot(h2b, wr_ref[...])
    lane = lax.broadcasted_iota(jnp.int32, logits.shape, 1)
    logits = jnp.where(lane < N_EXPERTS, logits, NEG_INF)
    e = jnp.exp(logits - jnp.max(logits, axis=1, keepdims=True))
    return e / jnp.sum(e, axis=1, keepdims=True)


def _outproj_core(o_ref, x_ref, g1_ref, lg_ref, lb_ref, sh_ref, sc_ref, wo_ref, wr_ref, alpha):
    mix = _dot(o_ref[...], wo_ref[...])
    z = alpha * x_ref[...] + g1_ref[...] * mix
    x_mid = _ln(z) * lg_ref[...] + lb_ref[...]
    h2 = _ln(x_mid) * (1.0 + sc_ref[...]) + sh_ref[...]
    return x_mid, h2, _router_affinity(h2.astype(BF16), wr_ref)


def _outproj_body(o_ref, x_ref, g1_ref, lg_ref, lb_ref, sh_ref, sc_ref, wo_ref, wr_ref,
                  xm_ref, h2_ref, aff_ref, afft_ref, *, alpha, packed):
    x_mid, h2, aff = _outproj_core(o_ref, x_ref, g1_ref, lg_ref, lb_ref, sh_ref, sc_ref, wo_ref, wr_ref, alpha)
    xm_ref[...] = x_mid
    if packed:
        half = h2.shape[1] // 2
        h2_ref[...] = pltpu.pack_elementwise([h2[:, :half], h2[:, half:]], packed_dtype=BF16)
    else:
        h2_ref[...] = h2.astype(BF16)
    aff_ref[...] = aff
    afft_ref[...] = aff.T[:N_EXPERTS, :]


def _outproj(o, x, g1, lg, lb, sh, sc, wo, wr, tm, alpha, latent):
    bsz, seq, d = x.shape
    tok = lambda width: pl.BlockSpec((None, tm, width), lambda b, i: (b, i, 0))
    vec = pl.BlockSpec((None, 1, d), lambda b, i: (b, 0, 0))
    par = pl.BlockSpec((1, d), lambda b, i: (0, 0))
    afft_spec = pl.BlockSpec((None, N_EXPERTS, tm), lambda b, i: (b, 0, i))
    afft_shape = jax.ShapeDtypeStruct((bsz, N_EXPERTS, seq), F32)
    h2_width, h2_dtype = (d // 2, jnp.uint32) if latent else (d, BF16)
    out_specs = [tok(d), tok(h2_width), tok(LANES), afft_spec]
    out_shape = [jax.ShapeDtypeStruct((bsz, seq, d), F32), jax.ShapeDtypeStruct((bsz, seq, h2_width), h2_dtype),
                 jax.ShapeDtypeStruct((bsz, seq, LANES), F32), afft_shape]
    return pl.pallas_call(
        functools.partial(_outproj_body, alpha=alpha, packed=latent),
        grid=(bsz, seq // tm),
        in_specs=[tok(d), tok(d), vec, par, par, vec, vec,
                  pl.BlockSpec(wo.shape, lambda b, i: (0, 0)), pl.BlockSpec(wr.shape, lambda b, i: (0, 0))],
        out_specs=out_specs,
        out_shape=out_shape,
        compiler_params=_cparams(VMEM_LIMIT),
        name="outproj_latent" if latent else "outproj_ctx",
    )(o, x, g1, lg, lb, sh, sc, wo, wr)


def _kth_largest_bucket(aff, cap, axes):
    shape = list(aff.shape)
    for a in axes:
        shape[a] = 1
    thr = jnp.zeros(shape, jnp.int32)
    for bit in range(30, -1, -1):
        cand = thr | jnp.int32(1 << bit)
        cnt = _count(aff >= lax.bitcast_convert_type(cand, F32), axes)
        thr = jnp.where(cnt >= cap, cand, thr)
    return lax.bitcast_convert_type(thr, F32), lax.bitcast_convert_type(thr + 1, F32)


def _count(mask, axes):
    c = jnp.where(mask, 1.0, 0.0)
    for a in sorted(axes, reverse=True):
        c = jnp.sum(c, axis=a, keepdims=True)
    return c


def _select_latent_body(a_ref, u_ref, lt_ref, lb_ref, idx_ref, *, cap, nc):
    aff = a_ref[...]
    rows = aff.shape[0]
    aff3 = aff.reshape(N_EXPERTS, nc, LANES)
    t_lo, t_hi = _kth_largest_bucket(aff3, cap, (1, 2))
    gt = aff3 >= t_hi
    eq = (aff3 >= t_lo) & (aff3 < t_hi)
    need = cap - _count(gt, (1, 2))
    eq_f = jnp.where(eq, 1.0, 0.0).reshape(rows, LANES)
    eq_b = eq_f.astype(BF16)
    rank = (_dot(eq_b, u_ref[...]) - eq_f
            + jnp.sum(_dot(lb_ref[...], eq_b), axis=1, keepdims=True))
    sel3 = gt | (eq & (rank.reshape(N_EXPERTS, nc, LANES) < need))
    sel_f = jnp.where(sel3, 1.0, 0.0).reshape(rows, LANES)
    sel_b = sel_f.astype(BF16)
    cnt = jnp.sum(sel_f, axis=1, keepdims=True)
    g_incl = jnp.sum(_dot(lb_ref[...], sel_b), axis=1, keepdims=True) + cnt
    r_row = lax.broadcasted_iota(jnp.int32, (1, cap), 1).astype(F32)
    c_col = lax.broadcasted_iota(jnp.int32, (nc, 1), 0).astype(F32)
    for e in range(N_EXPERTS):
        sl = slice(e * nc, (e + 1) * nc)
        before = g_incl[sl] <= r_row
        chunk = jnp.sum(jnp.where(before, 1.0, 0.0), axis=0, keepdims=True)
        base = jnp.sum(jnp.where(before, cnt[sl], 0.0), axis=0, keepdims=True)
        onehot = jnp.where(c_col == chunk, 1.0, 0.0).astype(BF16)
        cs_t = _dot_nt(lt_ref[...], sel_b[sl]).astype(BF16)
        cs_of_r = _dot(cs_t, onehot)
        local = jnp.sum(jnp.where(cs_of_r <= r_row - base, 1.0, 0.0), axis=0, keepdims=True)
        idx_ref[e:e + 1, :] = (chunk * LANES + local).astype(jnp.int32)


def _select_latent(aff_t, cap):
    bsz, n_exp, seq = aff_t.shape
    nc = seq // LANES
    rows = n_exp * nc
    tri = np.triu(np.ones((LANES, LANES), np.float32))
    r = np.arange(rows)
    blockdiag = ((r[:, None] // nc == r[None, :] // nc) & (r[None, :] < r[:, None])).astype(np.float32)
    const = lambda shape: pl.BlockSpec(shape, lambda b: (0, 0))
    return pl.pallas_call(
        functools.partial(_select_latent_body, cap=cap, nc=nc),
        grid=(bsz,),
        in_specs=[pl.BlockSpec((None, rows, LANES), lambda b: (b, 0, 0)),
                  const((LANES, LANES)), const((LANES, LANES)), const((rows, rows))],
        out_specs=pl.BlockSpec((None, n_exp, cap), lambda b: (b, 0, 0)),
        out_shape=jax.ShapeDtypeStruct((bsz, n_exp, cap), jnp.int32),
        compiler_params=_cparams(VMEM_LIMIT),
        name="select_latent",
    )(aff_t.reshape(bsz, rows, LANES), jnp.asarray(tri, BF16), jnp.asarray(tri.T, BF16),
      jnp.asarray(blockdiag, BF16))


def _select_ctx_body(a_ref, u_ref, w_ref, *, cap):
    aff = a_ref[...]
    t_lo, t_hi = _kth_largest_bucket(aff, cap, (1,))
    gt = aff >= t_hi
    eq = (aff >= t_lo) & (aff < t_hi)
    need = cap - _count(gt, (1,))
    eq_f = jnp.where(eq, 1.0, 0.0)
    rank = _dot(eq_f.astype(BF16), u_ref[...]) - eq_f
    w_ref[...] = jnp.where(gt | (eq & (rank < need)), aff, 0.0)


def _select_ctx(aff_t, cap):
    bsz, n_exp, lctx = aff_t.shape
    tri = np.triu(np.ones((lctx, lctx), np.float32))
    blk = pl.BlockSpec((None, n_exp, lctx), lambda b: (b, 0, 0))
    return pl.pallas_call(
        functools.partial(_select_ctx_body, cap=cap),
        grid=(bsz,),
        in_specs=[blk, pl.BlockSpec((lctx, lctx), lambda b: (0, 0))],
        out_specs=blk,
        out_shape=jax.ShapeDtypeStruct((bsz, n_exp, lctx), F32),
        name="select_ctx",
    )(aff_t, jnp.asarray(tri, BF16))


def _swiglu(xe, wg_ref, wu_ref, wd_ref):
    a = _dot(xe, wg_ref[...].astype(BF16))
    u = _dot(xe, wu_ref[...].astype(BF16))
    act = (a * jax.nn.sigmoid(a) * u).astype(BF16)
    return _dot(act, wd_ref[...].astype(BF16))


def _expert_gather_body(idx_ref, hp_ref, aff_ref, wg_ref, wu_ref, wd_ref, y_ref, xg_ref, ag_ref, *, cap):
    e = pl.program_id(1)

    def gather_row(r, carry):
        t = idx_ref[0, 0, r]
        xg_ref[pl.ds(r, 1), :] = hp_ref[pl.ds(t, 1), :]
        ag_ref[pl.ds(r, 1), :] = aff_ref[pl.ds(t, 1), :]
        return carry

    lax.fori_loop(0, cap, gather_row, 0, unroll=8)
    packed = xg_ref[...]
    halves = [pltpu.unpack_elementwise(packed, index=i, packed_dtype=BF16, unpacked_dtype=F32).astype(BF16)
              for i in range(2)]
    xe = jnp.concatenate(halves, axis=1)
    aff = ag_ref[...]
    lane = lax.broadcasted_iota(jnp.int32, aff.shape, 1)
    gate = jnp.sum(jnp.where(lane == e, aff, 0.0), axis=1, keepdims=True)
    y_ref[...] = _swiglu(xe, wg_ref, wu_ref, wd_ref) * gate


def _expert_gather(idx, hp, aff, wg, wu, wd):
    bsz, n_exp, cap = idx.shape
    seq, width = hp.shape[1:]
    d, ff = wg.shape[1:]
    return pl.pallas_call(
        functools.partial(_expert_gather_body, cap=cap),
        grid=(bsz, n_exp),
        in_specs=[
            pl.BlockSpec((1, 1, cap), lambda b, e: (b * n_exp + e, 0, 0), memory_space=pltpu.SMEM),
            pl.BlockSpec((None, seq, width), lambda b, e: (b, 0, 0), pipeline_mode=pl.Buffered(1)),
            pl.BlockSpec((None, seq, LANES), lambda b, e: (b, 0, 0), pipeline_mode=pl.Buffered(1)),
            pl.BlockSpec((None, d, ff), lambda b, e: (e, 0, 0)),
            pl.BlockSpec((None, d, ff), lambda b, e: (e, 0, 0)),
            pl.BlockSpec((None, ff, d), lambda b, e: (e, 0, 0)),
        ],
        out_specs=pl.BlockSpec((None, None, cap, d), lambda b, e: (b, e, 0, 0)),
        out_shape=jax.ShapeDtypeStruct((bsz, n_exp, cap, d), F32),
        scratch_shapes=[pltpu.VMEM((cap, width), jnp.uint32), pltpu.VMEM((cap, LANES), F32)],
        compiler_params=_cparams(VMEM_LIMIT),
        name="expert_gather",
    )(idx.reshape(bsz * n_exp, 1, cap), hp, aff, wg, wu, wd)


def _scatter_body(idx_ref, y_ref, o_ref, *, cap, span):
    s = pl.program_id(1)
    e = pl.program_id(2)

    @pl.when(e == 0)
    def _():
        o_ref[...] = jnp.zeros_like(o_ref)

    base = s * span

    def lower_bound(v):
        def step(_, lh):
            lo, hi = lh
            mid = (lo + hi) // 2
            below = idx_ref[0, 0, jnp.minimum(mid, cap - 1)] < v
            active = lo < hi
            return (jnp.where(active & below, mid + 1, lo), jnp.where(active & (~below), mid, hi))
        return lax.fori_loop(0, cap.bit_length() + 1, step, (jnp.int32(0), jnp.int32(cap)))[0]

    def add_row(r, carry):
        t = idx_ref[0, 0, r] - base
        o_ref[pl.ds(t, 1), :] = o_ref[pl.ds(t, 1), :] + y_ref[pl.ds(r, 1), :]
        return carry

    lax.fori_loop(lower_bound(base), lower_bound(base + span), add_row, 0)


def _scatter_add(idx, y, seq, n_split):
    bsz, n_exp, cap, d = y.shape
    span = seq // n_split
    return pl.pallas_call(
        functools.partial(_scatter_body, cap=cap, span=span),
        grid=(bsz, n_split, n_exp),
        in_specs=[
            pl.BlockSpec((1, 1, cap), lambda b, s, e: (b * n_exp + e, 0, 0), memory_space=pltpu.SMEM),
            pl.BlockSpec((None, None, cap, d), lambda b, s, e: (b, e, 0, 0)),
        ],
        out_specs=pl.BlockSpec((None, span, d), lambda b, s, e: (b, s, 0)),
        out_shape=jax.ShapeDtypeStruct((bsz, seq, d), F32),
        compiler_params=_cparams(VMEM_LIMIT),
        name="scatter_add",
    )(idx.reshape(bsz * n_exp, 1, cap), y)


def _expert_dense_body(h_ref, w_ref, wg_ref, wu_ref, wd_ref, o_ref):
    e = pl.program_id(0)

    @pl.when(e == 0)
    def _():
        o_ref[...] = jnp.zeros_like(o_ref)

    w = w_ref[...]
    lane = lax.broadcasted_iota(jnp.int32, w.shape, 1)
    gate = jnp.sum(jnp.where(lane == e, w, 0.0), axis=1, keepdims=True)
    o_ref[...] += _swiglu(h_ref[...], wg_ref, wu_ref, wd_ref) * gate


def _expert_dense(h, w, wg, wu, wd):
    t, d = h.shape
    n_exp, _, ff = wg.shape
    return pl.pallas_call(
        _expert_dense_body,
        grid=(n_exp,),
        in_specs=[
            pl.BlockSpec((t, d), lambda e: (0, 0)),
            pl.BlockSpec(w.shape, lambda e: (0, 0)),
            pl.BlockSpec((None, d, ff), lambda e: (e, 0, 0)),
            pl.BlockSpec((None, d, ff), lambda e: (e, 0, 0)),
            pl.BlockSpec((None, ff, d), lambda e: (e, 0, 0)),
        ],
        out_specs=pl.BlockSpec((t, d), lambda e: (0, 0)),
        out_shape=jax.ShapeDtypeStruct((t, d), F32),
        compiler_params=_cparams(VMEM_LIMIT),
        name="expert_dense",
    )(h, w, wg, wu, wd)


def _final_ln_body(xm_ref, f_ref, g2_ref, lg_ref, lb_ref, o_ref, *, alpha):
    z = alpha * xm_ref[...] + g2_ref[...] * f_ref[...]
    o_ref[...] = _ln(z) * lg_ref[...] + lb_ref[...]


def _final_ln(xm, ffn, g2, lg, lb, tm, alpha):
    bsz, seq, d = xm.shape
    tok = pl.BlockSpec((None, tm, d), lambda b, i: (b, i, 0))
    par = pl.BlockSpec((1, d), lambda b, i: (0, 0))
    return pl.pallas_call(
        functools.partial(_final_ln_body, alpha=alpha),
        grid=(bsz, seq // tm),
        in_specs=[tok, tok, pl.BlockSpec((None, 1, d), lambda b, i: (b, 0, 0)), par, par],
        out_specs=tok,
        out_shape=jax.ShapeDtypeStruct((bsz, seq, d), F32),
        compiler_params=_cparams(VMEM_LIMIT),
        name="final_ln",
    )(xm, ffn, g2, lg, lb)


def _rope_tables(seq):
    t = np.arange(seq)
    inv = ROPE_THETA ** (-np.arange(0, ROPE_AXIS_DIM, 2, dtype=np.float32) / ROPE_AXIS_DIM)
    inv = jnp.asarray(inv, F32)

    def axis(pos):
        ang = jnp.asarray(pos, F32)[:, None] * inv[None, :]
        c, s = jnp.cos(ang), jnp.sin(ang)
        return jnp.concatenate([c, c], axis=1), jnp.concatenate([-s, s], axis=1)

    cr, sr = axis(t // GRID_W)
    cc, sc = axis(t % GRID_W)
    reps = LANES // HEAD_DIM
    return (jnp.tile(jnp.concatenate([cr, cc], axis=1), (1, reps)),
            jnp.tile(jnp.concatenate([sr, sc], axis=1), (1, reps)))


def _token_tile(seq):
    return 512 if seq % 512 == 0 else seq


def kernel(x, c, ctx, c_ctx, w_mod, b_mod, w_in, conv_w, attn_sink, na_rpb, w_out, ln1_g, ln1_b,
           w_router, w_gate, w_up, w_down, ln2_g, ln2_b):
    bsz, seq, d = x.shape
    lctx = ctx.shape[1]
    depth = w_mod.shape[0]
    alpha = (2 * depth) ** 0.25
    nb = seq // A_BLOCK
    assert seq % A_BLOCK == 0 and nb >= WIN_BLOCKS + 1 and bsz + 1 <= MOD_ROWS
    tm = _token_tile(seq)
    tmc = _token_tile(bsz * lctx)

    cond = jnp.zeros((MOD_ROWS, d), F32).at[:bsz].set(c).at[bsz].set(c_ctx)
    mod = _modulation(cond, w_mod, b_mod)

    cos, sin = _rope_tables(seq)
    cos_c = jnp.ones((bsz * lctx, LANES), F32)
    sin_c = jnp.zeros((bsz * lctx, LANES), F32)

    head_order = [t * A_GROUP + j for j in range(A_GROUP) for t in range(A_KV_HEADS)]
    a_cols = np.concatenate([np.arange(h * HEAD_DIM, (h + 1) * HEAD_DIM) for h in head_order])
    in_cols = np.concatenate([a_cols, np.arange(A_WIDTH, IN_WIDTH)])
    out_rows = np.concatenate([a_cols, np.arange(A_WIDTH, w_out.shape[1])])

    cap = CAPACITY_FACTOR * seq // N_EXPERTS
    cap_c = CAPACITY_FACTOR * lctx // N_EXPERTS
    ctx_flat = ctx.reshape(1, bsz * lctx, d)

    for l in range(depth):
        last = l == depth - 1
        m6 = mod[l].reshape(MOD_ROWS, N_MOD, d)
        sh1, sc1, g1, sh2, sc2, g2 = (m6[:bsz, i].reshape(bsz, 1, d) for i in range(N_MOD))
        csh1, csc1, cg1, csh2, csc2, cg2 = (m6[bsz:bsz + 1, i].reshape(1, 1, d) for i in range(N_MOD))
        w_in_l = w_in[l][:, in_cols].astype(BF16)
        w_out_l = w_out[l][out_rows, :].astype(BF16)
        w_r = jnp.zeros((d, LANES), BF16).at[:, :N_EXPERTS].set(w_router[l].astype(BF16))
        sink = attn_sink[l]
        wg, wu, wd = w_gate[l].astype(BF16), w_up[l].astype(BF16), w_down[l].astype(BF16)
        lg1, lb1 = ln1_g[l].reshape(1, d), ln1_b[l].reshape(1, d)
        lg2, lb2 = ln2_g[l].reshape(1, d), ln2_b[l].reshape(1, d)

        qa, kva, ub, qn, kvn = _inproj(x, sh1, sc1, cos, sin, w_in_l, tm)
        qa_c, kva_c, ub_c, qn_c, kvn_c = (
            a.reshape(bsz, lctx, a.shape[-1])
            for a in _inproj(ctx_flat, csh1, csc1, cos_c, sin_c, w_in_l, tmc))
        bias = _na_bias_tables(na_rpb[l], nb)
        o = _mix_latent(sink, qa, kva, ub, qn, kvn, kva_c, kvn_c, bias, conv_w[l])
        x_mid, hp, aff, aff_t = _outproj(o, x, g1, lg1, lb1, sh2, sc2, w_out_l, w_r, tm, alpha, True)

        if not last:
            o_c = _mix_ctx(sink, qa_c, kva_c, ub_c, qn_c, kvn_c, conv_w[l])
            ctx_mid, h2c, aff_c, aff_ct = _outproj(
                o_c.reshape(1, bsz * lctx, d), ctx_flat, cg1, lg1, lb1, csh2, csc2, w_out_l, w_r,
                tmc, alpha, False)
            aff_ct = jnp.transpose(aff_ct.reshape(N_EXPERTS, bsz, lctx), (1, 0, 2))
            gate_c = _select_ctx(aff_ct, cap_c)
            gate_c = jnp.transpose(gate_c, (0, 2, 1)).reshape(bsz * lctx, N_EXPERTS)
            ffn_c = _expert_dense(h2c[0], gate_c, wg, wu, wd)
            ctx_flat = _final_ln(ctx_mid, ffn_c[None], cg2, lg2, lb2, tmc, alpha)

        idx = _select_latent(aff_t, cap)
        y = _expert_gather(idx, hp, aff, wg, wu, wd)
        ffn = _scatter_add(idx, y, seq, 2)
        x = _final_ln(x_mid, ffn, g2, lg2, lb2, tm, alpha)
    return x
```

```python
import functools

import numpy as np
import jax
import jax.numpy as jnp
from jax import lax
from jax.experimental import pallas as pl
from jax.experimental.pallas import tpu as pltpu

HEAD_DIM = 64
GRID_W = 64
A_Q_HEADS = 6
A_KV_HEADS = 2
A_GROUP = A_Q_HEADS // A_KV_HEADS
A_WINDOW = 128
A_BLOCK = 128
B_WIDTH = 256
C_HEADS = 6
NA_ROWS = 8
NA_COLS = 16
A_WIDTH = A_Q_HEADS * HEAD_DIM
A_KV_WIDTH = A_KV_HEADS * HEAD_DIM
C_WIDTH = C_HEADS * HEAD_DIM
N_EXPERTS = 16
CAPACITY_FACTOR = 2
ROPE_THETA = 10000.0
ROPE_AXIS_DIM = HEAD_DIM // 2
LN_EPS = 1e-6
N_MOD = 6
NEG_INF = -1e30

LANES = 128
SUBLANES = 8
MOD_ROWS = SUBLANES
HALO_ROWS = 16
WIN_BLOCKS = 5
N_BIAS_PATTERNS = 5
VMEM_LIMIT = 56 * 1024 * 1024

BF16 = jnp.bfloat16
F32 = jnp.float32


def _cparams(vmem=None):
    return pltpu.CompilerParams(vmem_limit_bytes=vmem) if vmem else None


def _dot(a, b):
    return jnp.dot(a, b, preferred_element_type=F32)


def _dot_nt(a, b):
    return lax.dot_general(a, b, (((1,), (1,)), ((), ())), preferred_element_type=F32)


def _ln(x):
    mu = jnp.mean(x, axis=-1, keepdims=True)
    xc = x - mu
    var = jnp.mean(xc * xc, axis=-1, keepdims=True)
    return xc * lax.rsqrt(var + LN_EPS)


def _mod_body(c_ref, w_ref, b_ref, o_ref):
    c = c_ref[...]
    h = (c * jax.nn.sigmoid(c)).astype(BF16)
    o_ref[...] = _dot(h, w_ref[...].astype(BF16)) + b_ref[...]


def _modulation(cond, w_mod, b_mod):
    depth, d, n = w_mod.shape
    tn = n // 4
    return pl.pallas_call(
        _mod_body,
        grid=(depth, n // tn),
        in_specs=[
            pl.BlockSpec((MOD_ROWS, d), lambda l, j: (0, 0)),
            pl.BlockSpec((None, d, tn), lambda l, j: (l, 0, j)),
            pl.BlockSpec((None, 1, tn), lambda l, j: (l, 0, j)),
        ],
        out_specs=pl.BlockSpec((None, MOD_ROWS, tn), lambda l, j: (l, 0, j)),
        out_shape=jax.ShapeDtypeStruct((depth, MOD_ROWS, n), F32),
        compiler_params=_cparams(VMEM_LIMIT),
        name="modulation",
    )(cond, w_mod, b_mod.reshape(depth, 1, n))


QK_W = A_WIDTH + HEAD_DIM * A_KV_HEADS
O_VA = QK_W
O_B = O_VA + A_KV_WIDTH
O_QN = O_B + 3 * B_WIDTH
O_KVN = O_QN + C_WIDTH
IN_WIDTH = O_KVN + 2 * C_WIDTH


def _inproj_body(x_ref, sh_ref, sc_ref, cos_ref, sin_ref, w_ref,
                 qa_ref, kva_ref, ub_ref, qn_ref, kvn_ref):
    h = _ln(x_ref[...]) * (1.0 + sc_ref[...]) + sh_ref[...]
    hb = h.astype(BF16)
    scale = HEAD_DIM ** -0.5
    qk = _dot(hb, w_ref[:, 0:QK_W])
    cos = cos_ref[...]
    sin = sin_ref[...]
    lane = lax.broadcasted_iota(jnp.int32, cos.shape, 1)
    first = (lane % ROPE_AXIS_DIM) < (ROPE_AXIS_DIM // 2)
    half = ROPE_AXIS_DIM // 2
    parts = []
    for j in range(QK_W // LANES):
        t = qk[:, j * LANES:(j + 1) * LANES]
        partner = jnp.where(first, pltpu.roll(t, LANES - half, 1), pltpu.roll(t, half, 1))
        parts.append(t * cos + partner * sin)
    qa_ref[...] = (jnp.concatenate(parts[:-1], axis=1) * scale).astype(BF16)
    va = _dot(hb, w_ref[:, O_VA:O_B])
    kva_ref[...] = jnp.concatenate([parts[-1], va], axis=1).astype(BF16)
    g = _dot(hb, w_ref[:, O_B:O_QN])
    bx, bb, bc = g[:, :B_WIDTH], g[:, B_WIDTH:2 * B_WIDTH], g[:, 2 * B_WIDTH:]
    ub_ref[...] = jnp.concatenate([bc * bx, bb], axis=1).astype(BF16)
    qn_ref[...] = (_dot(hb, w_ref[:, O_QN:O_KVN]) * scale).astype(BF16)
    kvn_ref[...] = _dot(hb, w_ref[:, O_KVN:IN_WIDTH]).astype(BF16)


def _inproj(x, sh, sc, cos, sin, w, tm):
    bsz, seq, d = x.shape
    tok = lambda width: pl.BlockSpec((None, tm, width), lambda b, i: (b, i, 0))
    vec = pl.BlockSpec((None, 1, d), lambda b, i: (b, 0, 0))
    tab = pl.BlockSpec((tm, LANES), lambda b, i: (i, 0))
    widths = (A_WIDTH, 2 * A_KV_WIDTH, 2 * B_WIDTH, C_WIDTH, 2 * C_WIDTH)
    return pl.pallas_call(
        _inproj_body,
        grid=(bsz, seq // tm),
        in_specs=[tok(d), vec, vec, tab, tab, pl.BlockSpec((d, IN_WIDTH), lambda b, i: (0, 0))],
        out_specs=[tok(wd) for wd in widths],
        out_shape=[jax.ShapeDtypeStruct((bsz, seq, wd), BF16) for wd in widths],
        compiler_params=_cparams(VMEM_LIMIT),
        name="inproj",
    )(x, sh, sc, cos, sin, w)


def _split_heads(q, lo):
    zero = jnp.zeros_like(q)
    return jnp.concatenate([jnp.where(lo, q, zero), jnp.where(lo, zero, q)], axis=0)


def _gqa(qa, k_cat, v_cat, sink_ref, mask, m_rows):
    lo = lax.broadcasted_iota(jnp.int32, (m_rows, LANES), 1) < HEAD_DIM
    tiles = A_WIDTH // LANES
    qs = jnp.concatenate([_split_heads(qa[:, j * LANES:(j + 1) * LANES], lo) for j in range(tiles)], axis=0)
    s = _dot_nt(qs, k_cat)
    es, dens = [], []
    for blk in range(2 * tiles):
        head = (blk % 2) * A_GROUP + blk // 2
        sb = s[blk * m_rows:(blk + 1) * m_rows]
        if mask is not None:
            sb = jnp.where(mask, sb, NEG_INF)
        sk = sink_ref[head]
        m = jnp.maximum(jnp.max(sb, axis=1, keepdims=True), sk)
        e = jnp.exp(sb - m)
        dens.append(jnp.sum(e, axis=1, keepdims=True) + jnp.exp(sk - m))
        es.append(e.astype(BF16))
    o = _dot(jnp.concatenate(es, axis=0), v_cat)
    outs = []
    for j in range(tiles):
        o_lo = o[(2 * j) * m_rows:(2 * j + 1) * m_rows] / dens[2 * j]
        o_hi = o[(2 * j + 1) * m_rows:(2 * j + 2) * m_rows] / dens[2 * j + 1]
        outs.append(jnp.where(lo, o_lo, o_hi))
    return jnp.concatenate(outs, axis=1)


def _mha_pair(qt, k_cat, v_cat, bias_pair, n_biased, m_rows):
    lo = lax.broadcasted_iota(jnp.int32, (m_rows, LANES), 1) < HEAD_DIM
    s = _dot_nt(_split_heads(qt, lo), k_cat)
    if bias_pair is not None:
        s_w = s[:, :n_biased] + bias_pair
        s_c = s[:, n_biased:]
        m = jnp.maximum(jnp.max(s_w, axis=1, keepdims=True), jnp.max(s_c, axis=1, keepdims=True))
        e_w = jnp.exp(s_w - m)
        e_c = jnp.exp(s_c - m)
        den = jnp.sum(e_w, axis=1, keepdims=True) + jnp.sum(e_c, axis=1, keepdims=True)
        e = jnp.concatenate([e_w.astype(BF16), e_c.astype(BF16)], axis=1)
    else:
        m = jnp.max(s, axis=1, keepdims=True)
        e32 = jnp.exp(s - m)
        den = jnp.sum(e32, axis=1, keepdims=True)
        e = e32.astype(BF16)
    o = _dot(e, v_cat) / den
    return jnp.where(lo, o[:m_rows], o[m_rows:])


def _short_conv(ub, prev_row, next_row, w_ref, m_rows):
    u = ub[:, :B_WIDTH].astype(F32)
    bb = ub[:, B_WIDTH:].astype(F32)
    row = lax.broadcasted_iota(jnp.int32, u.shape, 0)
    u_m1 = jnp.where(row == 0, prev_row, pltpu.roll(u, 1, 0))
    u_p1 = jnp.where(row == m_rows - 1, next_row, pltpu.roll(u, m_rows - 1, 0))
    y = u_m1 * w_ref[0:1, :] + u * w_ref[1:2, :] + u_p1 * w_ref[2:3, :]
    return bb * y


def _mix_latent_body(sink_ref, qa_ref, kp_ref, ko_ref, kn_ref, ubp_ref, ubo_ref, ubn_ref, qn_ref,
                     w0_ref, w1_ref, w2_ref, w3_ref, w4_ref, kvac_ref, kvnc_ref, bias_ref, cw_ref,
                     o_ref, *, seq):
    n = pl.program_id(1)
    nb = pl.num_programs(1)
    m_rows = A_BLOCK
    k_cat = jnp.concatenate([kp_ref[:, :A_KV_WIDTH], ko_ref[:, :A_KV_WIDTH], kn_ref[:, :A_KV_WIDTH],
                             kvac_ref[:, :A_KV_WIDTH]], axis=0)
    v_cat = jnp.concatenate([kp_ref[:, A_KV_WIDTH:], ko_ref[:, A_KV_WIDTH:], kn_ref[:, A_KV_WIDTH:],
                             kvac_ref[:, A_KV_WIDTH:]], axis=0)
    n_keys = k_cat.shape[0]
    qpos = n * A_BLOCK + lax.broadcasted_iota(jnp.int32, (m_rows, n_keys), 0)
    col = lax.broadcasted_iota(jnp.int32, (m_rows, n_keys), 1)
    kpos = (n - 1) * A_BLOCK + col
    mask = (col >= 3 * A_BLOCK) | ((jnp.abs(qpos - kpos) <= A_WINDOW) & (kpos >= 0) & (kpos < seq))
    o_a = _gqa(qa_ref[...], k_cat, v_cat, sink_ref, mask, m_rows)
    prev_row = jnp.where(n > 0, ubp_ref[HALO_ROWS - 1:HALO_ROWS, :B_WIDTH].astype(F32), 0.0)
    next_row = jnp.where(n < nb - 1, ubn_ref[0:1, :B_WIDTH].astype(F32), 0.0)
    o_b = _short_conv(ubo_ref[...], prev_row, next_row, cw_ref, m_rows)
    wins = (w0_ref, w1_ref, w2_ref, w3_ref, w4_ref)
    n_win = WIN_BLOCKS * A_BLOCK
    outs = []
    for j in range(C_WIDTH // LANES):
        ks = slice(j * LANES, (j + 1) * LANES)
        vs = slice(C_WIDTH + j * LANES, C_WIDTH + (j + 1) * LANES)
        kc = jnp.concatenate([w[:, ks] for w in wins] + [kvnc_ref[:, ks]], axis=0)
        vc = jnp.concatenate([w[:, vs] for w in wins] + [kvnc_ref[:, vs]], axis=0)
        bias_pair = jnp.concatenate([bias_ref[2 * j], bias_ref[2 * j + 1]], axis=0)
        outs.append(_mha_pair(qn_ref[:, ks], kc, vc, bias_pair, n_win, m_rows))
    o_c = jnp.concatenate(outs, axis=1)
    o_ref[...] = jnp.concatenate([o_a, o_b, o_c], axis=1).astype(BF16)


def _mix_latent(sink, qa, kva, ub, qn, kvn, kva_c, kvn_c, bias, conv_w):
    bsz, seq, _ = qa.shape
    nb = seq // A_BLOCK
    lctx = kva_c.shape[1]
    blk = A_BLOCK
    hb = blk // HALO_ROWS
    n_halo = seq // HALO_ROWS
    tok = lambda width, fn: pl.BlockSpec((None, blk, width), fn)
    own = lambda b, n: (b, n, 0)
    win = lambda i: (lambda b, n: (b, jnp.clip(n - 2, 0, nb - WIN_BLOCKS) + i, 0))
    pattern = lambda b, n: (jnp.minimum(n, 2) + jnp.maximum(n - (nb - 3), 0), 0, 0, 0)
    in_specs = [
        pl.BlockSpec(memory_space=pltpu.SMEM),
        tok(A_WIDTH, own),
        tok(2 * A_KV_WIDTH, lambda b, n: (b, jnp.maximum(n - 1, 0), 0)),
        tok(2 * A_KV_WIDTH, own),
        tok(2 * A_KV_WIDTH, lambda b, n: (b, jnp.minimum(n + 1, nb - 1), 0)),
        pl.BlockSpec((None, HALO_ROWS, 2 * B_WIDTH), lambda b, n: (b, jnp.maximum(n * hb - 1, 0), 0)),
        tok(2 * B_WIDTH, own),
        pl.BlockSpec((None, HALO_ROWS, 2 * B_WIDTH), lambda b, n: (b, jnp.minimum((n + 1) * hb, n_halo - 1), 0)),
        tok(C_WIDTH, own),
    ] + [tok(2 * C_WIDTH, win(i)) for i in range(WIN_BLOCKS)] + [
        pl.BlockSpec((None, lctx, 2 * A_KV_WIDTH), lambda b, n: (b, 0, 0)),
        pl.BlockSpec((None, lctx, 2 * C_WIDTH), lambda b, n: (b, 0, 0)),
        pl.BlockSpec((None, C_HEADS, blk, WIN_BLOCKS * blk), pattern),
        pl.BlockSpec(conv_w.shape, lambda b, n: (0, 0)),
    ]
    return pl.pallas_call(
        functools.partial(_mix_latent_body, seq=seq),
        grid=(bsz, nb),
        in_specs=in_specs,
        out_specs=tok(A_WIDTH + B_WIDTH + C_WIDTH, own),
        out_shape=jax.ShapeDtypeStruct((bsz, seq, A_WIDTH + B_WIDTH + C_WIDTH), BF16),
        compiler_params=_cparams(VMEM_LIMIT),
        name="mix_latent",
    )(sink, qa, kva, kva, kva, ub, ub, ub, qn, kvn, kvn, kvn, kvn, kvn, kva_c, kvn_c, bias, conv_w)


def _mix_ctx_body(sink_ref, qa_ref, kva_ref, ub_ref, qn_ref, kvn_ref, cw_ref, o_ref):
    m_rows = qa_ref.shape[0]
    o_a = _gqa(qa_ref[...], kva_ref[:, :A_KV_WIDTH], kva_ref[:, A_KV_WIDTH:], sink_ref, None, m_rows)
    o_b = _short_conv(ub_ref[...], 0.0, 0.0, cw_ref, m_rows)
    outs = []
    for j in range(C_WIDTH // LANES):
        ks = slice(j * LANES, (j + 1) * LANES)
        vs = slice(C_WIDTH + j * LANES, C_WIDTH + (j + 1) * LANES)
        outs.append(_mha_pair(qn_ref[:, ks], kvn_ref[:, ks], kvn_ref[:, vs], None, 0, m_rows))
    o_ref[...] = jnp.concatenate([o_a, o_b] + outs, axis=1).astype(BF16)


def _mix_ctx(sink, qa, kva, ub, qn, kvn, conv_w):
    bsz, lctx, _ = qa.shape
    full = lambda width: pl.BlockSpec((None, lctx, width), lambda b: (b, 0, 0))
    width = A_WIDTH + B_WIDTH + C_WIDTH
    return pl.pallas_call(
        _mix_ctx_body,
        grid=(bsz,),
        in_specs=[pl.BlockSpec(memory_space=pltpu.SMEM), full(A_WIDTH), full(2 * A_KV_WIDTH),
                  full(2 * B_WIDTH), full(C_WIDTH), full(2 * C_WIDTH),
                  pl.BlockSpec(conv_w.shape, lambda b: (0, 0))],
        out_specs=full(width),
        out_shape=jax.ShapeDtypeStruct((bsz, lctx, width), BF16),
        compiler_params=_cparams(VMEM_LIMIT),
        name="mix_ctx",
    )(sink, qa, kva, ub, qn, kvn, conv_w)


def _na_bias_tables(rpb, nb):
    per_blk = A_BLOCK // GRID_W
    rows = nb * per_blk
    kh = min(NA_ROWS, rows)
    qc = np.arange(GRID_W)[:, None]
    kc = np.arange(GRID_W)[None, :]
    coff = np.clip(kc - qc, -(NA_COLS - 1), NA_COLS - 1) + (NA_COLS - 1)
    cs = np.clip(qc - NA_COLS // 2, 0, GRID_W - NA_COLS)
    col_ok = (kc >= cs) & (kc < cs + NA_COLS)
    onehot = (coff[None] == np.arange(2 * NA_COLS - 1)[:, None, None]).astype(np.float32)
    tiles = jnp.einsum('hrc,cqk->hrqk', rpb, onehot, precision=lax.Precision.HIGHEST)
    tiles = jnp.where(col_ok, tiles, NEG_INF)
    masked = jnp.full((rpb.shape[0], GRID_W, GRID_W), NEG_INF, F32)
    patterns = []
    for blk in (0, 1, 2, nb - 2, nb - 1):
        wb = min(max(blk - 2, 0), nb - WIN_BLOCKS)
        q_rows = []
        for qi in range(per_blk):
            qr = per_blk * blk + qi
            rs = min(max(qr - kh // 2, 0), rows - kh)
            k_tiles = []
            for kj in range(WIN_BLOCKS * per_blk):
                kr = per_blk * wb + kj
                k_tiles.append(tiles[:, kr - qr + NA_ROWS - 1] if rs <= kr < rs + kh else masked)
            q_rows.append(jnp.concatenate(k_tiles, axis=-1))
        patterns.append(jnp.concatenate(q_rows, axis=-2))
    return jnp.stack(patterns)


def _router_affinity(h2b, wr_ref):
    logits = _dot(h2b, wr_ref[...])
    lane = lax.broadcasted_iota(jnp.int32, logits.shape, 1)
    logits = jnp.where(lane < N_EXPERTS, logits, NEG_INF)
    e = jnp.exp(logits - jnp.max(logits, axis=1, keepdims=True))
    return e / jnp.sum(e, axis=1, keepdims=True)


def _outproj_core(o_ref, x_ref, g1_ref, lg_ref, lb_ref, sh_ref, sc_ref, wo_ref, wr_ref, alpha):
    mix = _dot(o_ref[...], wo_ref[...])
    z = alpha * x_ref[...] + g1_ref[...] * mix
    x_mid = _ln(z) * lg_ref[...] + lb_ref[...]
    h2 = _ln(x_mid) * (1.0 + sc_ref[...]) + sh_ref[...]
    return x_mid, h2, _router_affinity(h2.astype(BF16), wr_ref)


def _outproj_body(o_ref, x_ref, g1_ref, lg_ref, lb_ref, sh_ref, sc_ref, wo_ref, wr_ref,
                  xm_ref, h2_ref, afft_ref, *, alpha, packed):
    x_mid, h2, aff = _outproj_core(o_ref, x_ref, g1_ref, lg_ref, lb_ref, sh_ref, sc_ref, wo_ref, wr_ref, alpha)
    xm_ref[...] = x_mid
    if packed:
        half = h2.shape[1] // 2
        h2_ref[...] = pltpu.pack_elementwise([h2[:, :half], h2[:, half:]], packed_dtype=BF16)
    else:
        h2_ref[...] = h2.astype(BF16)
    afft_ref[...] = aff.T[:N_EXPERTS, :]


def _outproj(o, x, g1, lg, lb, sh, sc, wo, wr, tm, alpha, latent):
    bsz, seq, d = x.shape
    tok = lambda width: pl.BlockSpec((None, tm, width), lambda b, i: (b, i, 0))
    vec = pl.BlockSpec((None, 1, d), lambda b, i: (b, 0, 0))
    par = pl.BlockSpec((1, d), lambda b, i: (0, 0))
    afft_spec = pl.BlockSpec((None, N_EXPERTS, tm), lambda b, i: (b, 0, i))
    afft_shape = jax.ShapeDtypeStruct((bsz, N_EXPERTS, seq), F32)
    h2_width, h2_dtype = (d // 2, jnp.uint32) if latent else (d, BF16)
    out_specs = [tok(d), tok(h2_width), afft_spec]
    out_shape = [jax.ShapeDtypeStruct((bsz, seq, d), F32), jax.ShapeDtypeStruct((bsz, seq, h2_width), h2_dtype),
                 afft_shape]
    return pl.pallas_call(
        functools.partial(_outproj_body, alpha=alpha, packed=latent),
        grid=(bsz, seq // tm),
        in_specs=[tok(d), tok(d), vec, par, par, vec, vec,
                  pl.BlockSpec(wo.shape, lambda b, i: (0, 0)), pl.BlockSpec(wr.shape, lambda b, i: (0, 0))],
        out_specs=out_specs,
        out_shape=out_shape,
        compiler_params=_cparams(VMEM_LIMIT),
        name="outproj_latent" if latent else "outproj_ctx",
    )(o, x, g1, lg, lb, sh, sc, wo, wr)


def _kth_largest_bucket(aff, cap, axes):
    shape = list(aff.shape)
    for a in axes:
        shape[a] = 1
    thr = jnp.zeros(shape, jnp.int32)
    for bit in range(30, -1, -1):
        cand = thr | jnp.int32(1 << bit)
        cnt = _count(aff >= lax.bitcast_convert_type(cand, F32), axes)
        thr = jnp.where(cnt >= cap, cand, thr)
    return lax.bitcast_convert_type(thr, F32), lax.bitcast_convert_type(thr + 1, F32)


def _count(mask, axes):
    c = jnp.where(mask, 1.0, 0.0)
    for a in sorted(axes, reverse=True):
        c = jnp.sum(c, axis=a, keepdims=True)
    return c


def _select_latent_body(a_ref, u_ref, lt_ref, lb_ref, idx_ref, gate_ref, *, cap, nc):
    aff = a_ref[...]
    rows = aff.shape[0]
    aff3 = aff.reshape(N_EXPERTS, nc, LANES)
    t_lo, t_hi = _kth_largest_bucket(aff3, cap, (1, 2))
    gt = aff3 >= t_hi
    eq = (aff3 >= t_lo) & (aff3 < t_hi)
    need = cap - _count(gt, (1, 2))
    eq_f = jnp.where(eq, 1.0, 0.0).reshape(rows, LANES)
    eq_b = eq_f.astype(BF16)
    rank = (_dot(eq_b, u_ref[...]) - eq_f
            + jnp.sum(_dot(lb_ref[...], eq_b), axis=1, keepdims=True))
    sel3 = gt | (eq & (rank.reshape(N_EXPERTS, nc, LANES) < need))
    sel_f = jnp.where(sel3, 1.0, 0.0).reshape(rows, LANES)
    sel_b = sel_f.astype(BF16)
    cnt = jnp.sum(sel_f, axis=1, keepdims=True)
    g_incl = jnp.sum(_dot(lb_ref[...], sel_b), axis=1, keepdims=True) + cnt
    r_row = lax.broadcasted_iota(jnp.int32, (1, cap), 1).astype(F32)
    c_col = lax.broadcasted_iota(jnp.int32, (nc, 1), 0).astype(F32)
    l_col = lax.broadcasted_iota(jnp.int32, (LANES, 1), 0).astype(F32)
    for e in range(N_EXPERTS):
        sl = slice(e * nc, (e + 1) * nc)
        before = g_incl[sl] <= r_row
        chunk = jnp.sum(jnp.where(before, 1.0, 0.0), axis=0, keepdims=True)
        base = jnp.sum(jnp.where(before, cnt[sl], 0.0), axis=0, keepdims=True)
        onehot = jnp.where(c_col == chunk, 1.0, 0.0)
        cs_t = _dot_nt(lt_ref[...], sel_b[sl]).astype(BF16)
        cs_of_r = _dot(cs_t, onehot.astype(BF16))
        local = jnp.sum(jnp.where(cs_of_r <= r_row - base, 1.0, 0.0), axis=0, keepdims=True)
        idx_ref[e:e + 1, :] = (chunk * LANES + local).astype(jnp.int32)
        aff_of_r = jnp.dot(aff[sl].T, onehot, precision=lax.Precision.HIGHEST, preferred_element_type=F32)
        gate_ref[e:e + 1, :] = jnp.sum(jnp.where(l_col == local, aff_of_r, 0.0), axis=0, keepdims=True)


def _select_latent(aff_t, cap):
    bsz, n_exp, seq = aff_t.shape
    nc = seq // LANES
    rows = n_exp * nc
    tri = np.triu(np.ones((LANES, LANES), np.float32))
    r = np.arange(rows)
    blockdiag = ((r[:, None] // nc == r[None, :] // nc) & (r[None, :] < r[:, None])).astype(np.float32)
    const = lambda shape: pl.BlockSpec(shape, lambda b: (0, 0))
    return pl.pallas_call(
        functools.partial(_select_latent_body, cap=cap, nc=nc),
        grid=(bsz,),
        in_specs=[pl.BlockSpec((None, rows, LANES), lambda b: (b, 0, 0)),
                  const((LANES, LANES)), const((LANES, LANES)), const((rows, rows))],
        out_specs=[pl.BlockSpec((None, n_exp, cap), lambda b: (b, 0, 0))] * 2,
        out_shape=[jax.ShapeDtypeStruct((bsz, n_exp, cap), jnp.int32),
                   jax.ShapeDtypeStruct((bsz, n_exp, cap), F32)],
        compiler_params=_cparams(VMEM_LIMIT),
        name="select_latent",
    )(aff_t.reshape(bsz, rows, LANES), jnp.asarray(tri, BF16), jnp.asarray(tri.T, BF16),
      jnp.asarray(blockdiag, BF16))


def _select_ctx_body(a_ref, u_ref, w_ref, *, cap):
    aff = a_ref[...]
    t_lo, t_hi = _kth_largest_bucket(aff, cap, (1,))
    gt = aff >= t_hi
    eq = (aff >= t_lo) & (aff < t_hi)
    need = cap - _count(gt, (1,))
    eq_f = jnp.where(eq, 1.0, 0.0)
    rank = _dot(eq_f.astype(BF16), u_ref[...]) - eq_f
    w_ref[...] = jnp.where(gt | (eq & (rank < need)), aff, 0.0)


def _select_ctx(aff_t, cap):
    bsz, n_exp, lctx = aff_t.shape
    tri = np.triu(np.ones((lctx, lctx), np.float32))
    blk = pl.BlockSpec((None, n_exp, lctx), lambda b: (b, 0, 0))
    return pl.pallas_call(
        functools.partial(_select_ctx_body, cap=cap),
        grid=(bsz,),
        in_specs=[blk, pl.BlockSpec((lctx, lctx), lambda b: (0, 0))],
        out_specs=blk,
        out_shape=jax.ShapeDtypeStruct((bsz, n_exp, lctx), F32),
        name="select_ctx",
    )(aff_t, jnp.asarray(tri, BF16))


def _swiglu(xe, wg_ref, wu_ref, wd_ref):
    a = _dot(xe, wg_ref[...].astype(BF16))
    u = _dot(xe, wu_ref[...].astype(BF16))
    act = (a * jax.nn.sigmoid(a) * u).astype(BF16)
    return _dot(act, wd_ref[...].astype(BF16))


def _expert_gather_body(idx_ref, hp_ref, wg_ref, wu_ref, wd_ref, y_ref, xg_ref, *, cap):
    def gather_group(g, carry):
        for k in range(SUBLANES):
            xg_ref[g, k:k + 1, :] = hp_ref[pl.ds(idx_ref[0, 0, g * SUBLANES + k], 1), :]
        return carry

    lax.fori_loop(0, cap // SUBLANES, gather_group, 0)
    packed = xg_ref[...].reshape(cap, xg_ref.shape[-1])
    halves = [pltpu.unpack_elementwise(packed, index=i, packed_dtype=BF16, unpacked_dtype=F32).astype(BF16)
              for i in range(2)]
    y_ref[...] = _swiglu(jnp.concatenate(halves, axis=1), wg_ref, wu_ref, wd_ref)


def _expert_gather(idx, hp, wg, wu, wd):
    bsz, n_exp, cap = idx.shape
    seq, width = hp.shape[1:]
    d, ff = wg.shape[1:]
    return pl.pallas_call(
        functools.partial(_expert_gather_body, cap=cap),
        grid=(bsz, n_exp),
        in_specs=[
            pl.BlockSpec((1, 1, cap), lambda b, e: (b * n_exp + e, 0, 0), memory_space=pltpu.SMEM),
            pl.BlockSpec((None, seq, width), lambda b, e: (b, 0, 0), pipeline_mode=pl.Buffered(1)),
            pl.BlockSpec((None, d, ff), lambda b, e: (e, 0, 0)),
            pl.BlockSpec((None, d, ff), lambda b, e: (e, 0, 0)),
            pl.BlockSpec((None, ff, d), lambda b, e: (e, 0, 0)),
        ],
        out_specs=pl.BlockSpec((None, None, cap, d), lambda b, e: (b, e, 0, 0)),
        out_shape=jax.ShapeDtypeStruct((bsz, n_exp, cap, d), F32),
        scratch_shapes=[pltpu.VMEM((cap // SUBLANES, SUBLANES, width), jnp.uint32)],
        compiler_params=_cparams(VMEM_LIMIT),
        name="expert_gather",
    )(idx.reshape(bsz * n_exp, 1, cap), hp, wg, wu, wd)


def _scatter_body(idx_ref, gate_ref, y_ref, o_ref, *, cap):
    @pl.when(pl.program_id(1) == 0)
    def _():
        o_ref[...] = jnp.zeros_like(o_ref)

    def add_group(g, carry):
        r0 = g * SUBLANES
        ts = [idx_ref[0, 0, r0 + k] for k in range(SUBLANES)]
        acc = [o_ref[pl.ds(ts[k], 1), :] + gate_ref[0, 0, r0 + k] * y_ref[g, k:k + 1, :]
               for k in range(SUBLANES)]
        for k in range(SUBLANES):
            o_ref[pl.ds(ts[k], 1), :] = acc[k]
        return carry

    lax.fori_loop(0, cap // SUBLANES, add_group, 0)


def _scatter_add(idx, gate, y, seq):
    bsz, n_exp, cap, d = y.shape
    sel = lambda b, e: (b * n_exp + e, 0, 0)
    return pl.pallas_call(
        functools.partial(_scatter_body, cap=cap),
        grid=(bsz, n_exp),
        in_specs=[
            pl.BlockSpec((1, 1, cap), sel, memory_space=pltpu.SMEM),
            pl.BlockSpec((1, 1, cap), sel, memory_space=pltpu.SMEM),
            pl.BlockSpec((None, None, cap // SUBLANES, SUBLANES, d), lambda b, e: (b, e, 0, 0, 0)),
        ],
        out_specs=pl.BlockSpec((None, seq, d), lambda b, e: (b, 0, 0), pipeline_mode=pl.Buffered(1)),
        out_shape=jax.ShapeDtypeStruct((bsz, seq, d), F32),
        compiler_params=_cparams(VMEM_LIMIT),
        name="scatter_add",
    )(idx.reshape(bsz * n_exp, 1, cap), gate.reshape(bsz * n_exp, 1, cap),
      y.reshape(bsz, n_exp, cap // SUBLANES, SUBLANES, d))


def _expert_dense_body(h_ref, w_ref, wg_ref, wu_ref, wd_ref, o_ref):
    e = pl.program_id(0)

    @pl.when(e == 0)
    def _():
        o_ref[...] = jnp.zeros_like(o_ref)

    w = w_ref[...]
    lane = lax.broadcasted_iota(jnp.int32, w.shape, 1)
    gate = jnp.sum(jnp.where(lane == e, w, 0.0), axis=1, keepdims=True)
    o_ref[...] += _swiglu(h_ref[...], wg_ref, wu_ref, wd_ref) * gate


def _expert_dense(h, w, wg, wu, wd):
    t, d = h.shape
    n_exp, _, ff = wg.shape
    return pl.pallas_call(
        _expert_dense_body,
        grid=(n_exp,),
        in_specs=[
            pl.BlockSpec((t, d), lambda e: (0, 0)),
            pl.BlockSpec(w.shape, lambda e: (0, 0)),
            pl.BlockSpec((None, d, ff), lambda e: (e, 0, 0)),
            pl.BlockSpec((None, d, ff), lambda e: (e, 0, 0)),
            pl.BlockSpec((None, ff, d), lambda e: (e, 0, 0)),
        ],
        out_specs=pl.BlockSpec((t, d), lambda e: (0, 0)),
        out_shape=jax.ShapeDtypeStruct((t, d), F32),
        compiler_params=_cparams(VMEM_LIMIT),
        name="expert_dense",
    )(h, w, wg, wu, wd)


def _final_ln_body(xm_ref, f_ref, g2_ref, lg_ref, lb_ref, o_ref, *, alpha):
    z = alpha * xm_ref[...] + g2_ref[...] * f_ref[...]
    o_ref[...] = _ln(z) * lg_ref[...] + lb_ref[...]


def _final_ln(xm, ffn, g2, lg, lb, tm, alpha):
    bsz, seq, d = xm.shape
    tok = pl.BlockSpec((None, tm, d), lambda b, i: (b, i, 0))
    par = pl.BlockSpec((1, d), lambda b, i: (0, 0))
    return pl.pallas_call(
        functools.partial(_final_ln_body, alpha=alpha),
        grid=(bsz, seq // tm),
        in_specs=[tok, tok, pl.BlockSpec((None, 1, d), lambda b, i: (b, 0, 0)), par, par],
        out_specs=tok,
        out_shape=jax.ShapeDtypeStruct((bsz, seq, d), F32),
        compiler_params=_cparams(VMEM_LIMIT),
        name="final_ln",
    )(xm, ffn, g2, lg, lb)


def _rope_tables(seq):
    t = np.arange(seq)
    inv = ROPE_THETA ** (-np.arange(0, ROPE_AXIS_DIM, 2, dtype=np.float32) / ROPE_AXIS_DIM)
    inv = jnp.asarray(inv, F32)

    def axis(pos):
        ang = jnp.asarray(pos, F32)[:, None] * inv[None, :]
        c, s = jnp.cos(ang), jnp.sin(ang)
        return jnp.concatenate([c, c], axis=1), jnp.concatenate([-s, s], axis=1)

    cr, sr = axis(t // GRID_W)
    cc, sc = axis(t % GRID_W)
    reps = LANES // HEAD_DIM
    return (jnp.tile(jnp.concatenate([cr, cc], axis=1), (1, reps)),
            jnp.tile(jnp.concatenate([sr, sc], axis=1), (1, reps)))


def _token_tile(seq):
    return 512 if seq % 512 == 0 else seq


def kernel(x, c, ctx, c_ctx, w_mod, b_mod, w_in, conv_w, attn_sink, na_rpb, w_out, ln1_g, ln1_b,
           w_router, w_gate, w_up, w_down, ln2_g, ln2_b):
    bsz, seq, d = x.shape
    lctx = ctx.shape[1]
    depth = w_mod.shape[0]
    alpha = (2 * depth) ** 0.25
    nb = seq // A_BLOCK
    assert seq % A_BLOCK == 0 and nb >= WIN_BLOCKS + 1 and bsz + 1 <= MOD_ROWS
    tm = _token_tile(seq)
    tmc = _token_tile(bsz * lctx)

    cond = jnp.zeros((MOD_ROWS, d), F32).at[:bsz].set(c).at[bsz].set(c_ctx)
    mod = _modulation(cond, w_mod, b_mod)

    cos, sin = _rope_tables(seq)
    cos_c = jnp.ones((bsz * lctx, LANES), F32)
    sin_c = jnp.zeros((bsz * lctx, LANES), F32)

    head_order = [t * A_GROUP + j for j in range(A_GROUP) for t in range(A_KV_HEADS)]
    a_cols = np.concatenate([np.arange(h * HEAD_DIM, (h + 1) * HEAD_DIM) for h in head_order])
    in_cols = np.concatenate([a_cols, np.arange(A_WIDTH, IN_WIDTH)])
    out_rows = np.concatenate([a_cols, np.arange(A_WIDTH, w_out.shape[1])])

    cap = CAPACITY_FACTOR * seq // N_EXPERTS
    cap_c = CAPACITY_FACTOR * lctx // N_EXPERTS
    ctx_flat = ctx.reshape(1, bsz * lctx, d)

    for l in range(depth):
        last = l == depth - 1
        m6 = mod[l].reshape(MOD_ROWS, N_MOD, d)
        sh1, sc1, g1, sh2, sc2, g2 = (m6[:bsz, i].reshape(bsz, 1, d) for i in range(N_MOD))
        csh1, csc1, cg1, csh2, csc2, cg2 = (m6[bsz:bsz + 1, i].reshape(1, 1, d) for i in range(N_MOD))
        w_in_l = w_in[l][:, in_cols].astype(BF16)
        w_out_l = w_out[l][out_rows, :].astype(BF16)
        w_r = jnp.zeros((d, LANES), BF16).at[:, :N_EXPERTS].set(w_router[l].astype(BF16))
        sink = attn_sink[l]
        wg, wu, wd = w_gate[l].astype(BF16), w_up[l].astype(BF16), w_down[l].astype(BF16)
        lg1, lb1 = ln1_g[l].reshape(1, d), ln1_b[l].reshape(1, d)
        lg2, lb2 = ln2_g[l].reshape(1, d), ln2_b[l].reshape(1, d)

        qa, kva, ub, qn, kvn = _inproj(x, sh1, sc1, cos, sin, w_in_l, tm)
        qa_c, kva_c, ub_c, qn_c, kvn_c = (
            a.reshape(bsz, lctx, a.shape[-1])
            for a in _inproj(ctx_flat, csh1, csc1, cos_c, sin_c, w_in_l, tmc))
        bias = _na_bias_tables(na_rpb[l], nb)
        o = _mix_latent(sink, qa, kva, ub, qn, kvn, kva_c, kvn_c, bias, conv_w[l])
        x_mid, hp, aff_t = _outproj(o, x, g1, lg1, lb1, sh2, sc2, w_out_l, w_r, tm, alpha, True)

        if not last:
            o_c = _mix_ctx(sink, qa_c, kva_c, ub_c, qn_c, kvn_c, conv_w[l])
            ctx_mid, h2c, aff_ct = _outproj(
                o_c.reshape(1, bsz * lctx, d), ctx_flat, cg1, lg1, lb1, csh2, csc2, w_out_l, w_r,
                tmc, alpha, False)
            aff_ct = jnp.transpose(aff_ct.reshape(N_EXPERTS, bsz, lctx), (1, 0, 2))
            gate_c = _select_ctx(aff_ct, cap_c)
            gate_c = jnp.transpose(gate_c, (0, 2, 1)).reshape(bsz * lctx, N_EXPERTS)
            ffn_c = _expert_dense(h2c[0], gate_c, wg, wu, wd)
            ctx_flat = _final_ln(ctx_mid, ffn_c[None], cg2, lg2, lb2, tmc, alpha)

        idx, gate = _select_latent(aff_t, cap)
        y = _expert_gather(idx, hp, wg, wu, wd)
        ffn = _scatter_add(idx, gate, y, seq)
        x = _final_ln(x_mid, ffn, g2, lg2, lb2, tm, alpha)
    return x
```

```python
import functools

import numpy as np
import jax
import jax.numpy as jnp
from jax import lax
from jax.experimental import pallas as pl
from jax.experimental.pallas import tpu as pltpu

HEAD_DIM = 64
GRID_W = 64
A_Q_HEADS = 6
A_KV_HEADS = 2
A_GROUP = A_Q_HEADS // A_KV_HEADS
A_WINDOW = 128
A_BLOCK = 128
B_WIDTH = 256
C_HEADS = 6
NA_ROWS = 8
NA_COLS = 16
A_WIDTH = A_Q_HEADS * HEAD_DIM
A_KV_WIDTH = A_KV_HEADS * HEAD_DIM
C_WIDTH = C_HEADS * HEAD_DIM
N_EXPERTS = 16
CAPACITY_FACTOR = 2
ROPE_THETA = 10000.0
ROPE_AXIS_DIM = HEAD_DIM // 2
LN_EPS = 1e-6
N_MOD = 6
NEG_INF = -1e30

LANES = 128
SUBLANES = 8
MOD_ROWS = SUBLANES
HALO_ROWS = 16
WIN_BLOCKS = 5
N_BIAS_PATTERNS = 5
VMEM_LIMIT = 56 * 1024 * 1024

BF16 = jnp.bfloat16
F32 = jnp.float32


def _cparams(vmem=None):
    return pltpu.CompilerParams(vmem_limit_bytes=vmem) if vmem else None


def _dot(a, b):
    return jnp.dot(a, b, preferred_element_type=F32)


def _dot_nt(a, b):
    return lax.dot_general(a, b, (((1,), (1,)), ((), ())), preferred_element_type=F32)


def _ln(x):
    mu = jnp.mean(x, axis=-1, keepdims=True)
    xc = x - mu
    var = jnp.mean(xc * xc, axis=-1, keepdims=True)
    return xc * lax.rsqrt(var + LN_EPS)


def _mod_body(c_ref, w_ref, b_ref, o_ref):
    c = c_ref[...]
    h = (c * jax.nn.sigmoid(c)).astype(BF16)
    o_ref[...] = _dot(h, w_ref[...].astype(BF16)) + b_ref[...]


def _modulation(cond, w_mod, b_mod):
    depth, d, n = w_mod.shape
    tn = n // 4
    return pl.pallas_call(
        _mod_body,
        grid=(depth, n // tn),
        in_specs=[
            pl.BlockSpec((MOD_ROWS, d), lambda l, j: (0, 0)),
            pl.BlockSpec((None, d, tn), lambda l, j: (l, 0, j)),
            pl.BlockSpec((None, 1, tn), lambda l, j: (l, 0, j)),
        ],
        out_specs=pl.BlockSpec((None, MOD_ROWS, tn), lambda l, j: (l, 0, j)),
        out_shape=jax.ShapeDtypeStruct((depth, MOD_ROWS, n), F32),
        compiler_params=_cparams(VMEM_LIMIT),
        name="modulation",
    )(cond, w_mod, b_mod.reshape(depth, 1, n))


QK_W = A_WIDTH + HEAD_DIM * A_KV_HEADS
O_VA = QK_W
O_B = O_VA + A_KV_WIDTH
O_QN = O_B + 3 * B_WIDTH
O_KVN = O_QN + C_WIDTH
IN_WIDTH = O_KVN + 2 * C_WIDTH


def _inproj_body(x_ref, sh_ref, sc_ref, cos_ref, sin_ref, w_ref,
                 qa_ref, kva_ref, ub_ref, qn_ref, kvn_ref):
    h = _ln(x_ref[...]) * (1.0 + sc_ref[...]) + sh_ref[...]
    hb = h.astype(BF16)
    scale = HEAD_DIM ** -0.5
    qk = _dot(hb, w_ref[:, 0:QK_W])
    cos = cos_ref[...]
    sin = sin_ref[...]
    lane = lax.broadcasted_iota(jnp.int32, cos.shape, 1)
    first = (lane % ROPE_AXIS_DIM) < (ROPE_AXIS_DIM // 2)
    half = ROPE_AXIS_DIM // 2
    parts = []
    for j in range(QK_W // LANES):
        t = qk[:, j * LANES:(j + 1) * LANES]
        partner = jnp.where(first, pltpu.roll(t, LANES - half, 1), pltpu.roll(t, half, 1))
        parts.append(t * cos + partner * sin)
    qa_ref[...] = (jnp.concatenate(parts[:-1], axis=1) * scale).astype(BF16)
    va = _dot(hb, w_ref[:, O_VA:O_B])
    kva_ref[...] = jnp.concatenate([parts[-1], va], axis=1).astype(BF16)
    g = _dot(hb, w_ref[:, O_B:O_QN])
    bx, bb, bc = g[:, :B_WIDTH], g[:, B_WIDTH:2 * B_WIDTH], g[:, 2 * B_WIDTH:]
    ub_ref[...] = jnp.concatenate([bc * bx, bb], axis=1).astype(BF16)
    qn_ref[...] = (_dot(hb, w_ref[:, O_QN:O_KVN]) * scale).astype(BF16)
    kvn_ref[...] = _dot(hb, w_ref[:, O_KVN:IN_WIDTH]).astype(BF16)


def _inproj(x, sh, sc, cos, sin, w, tm):
    bsz, seq, d = x.shape
    tok = lambda width: pl.BlockSpec((None, tm, width), lambda b, i: (b, i, 0))
    vec = pl.BlockSpec((None, 1, d), lambda b, i: (b, 0, 0))
    tab = pl.BlockSpec((tm, LANES), lambda b, i: (i, 0))
    widths = (A_WIDTH, 2 * A_KV_WIDTH, 2 * B_WIDTH, C_WIDTH, 2 * C_WIDTH)
    return pl.pallas_call(
        _inproj_body,
        grid=(bsz, seq // tm),
        in_specs=[tok(d), vec, vec, tab, tab, pl.BlockSpec((d, IN_WIDTH), lambda b, i: (0, 0))],
        out_specs=[tok(wd) for wd in widths],
        out_shape=[jax.ShapeDtypeStruct((bsz, seq, wd), BF16) for wd in widths],
        compiler_params=_cparams(VMEM_LIMIT),
        name="inproj",
    )(x, sh, sc, cos, sin, w)


def _split_heads(q, lo):
    zero = jnp.zeros_like(q)
    return jnp.concatenate([jnp.where(lo, q, zero), jnp.where(lo, zero, q)], axis=0)


def _with_ones(v_cat):
    return jnp.concatenate([v_cat, jnp.ones_like(v_cat)], axis=1)


def _gqa(qa, k_cat, v_cat, sink_ref, mask, m_rows):
    lo = lax.broadcasted_iota(jnp.int32, (m_rows, LANES), 1) < HEAD_DIM
    tiles = A_WIDTH // LANES
    qs = jnp.concatenate([_split_heads(qa[:, j * LANES:(j + 1) * LANES], lo) for j in range(tiles)], axis=0)
    s = _dot_nt(qs, k_cat)
    es, sinks = [], []
    for blk in range(2 * tiles):
        head = (blk % 2) * A_GROUP + blk // 2
        sb = s[blk * m_rows:(blk + 1) * m_rows]
        if mask is not None:
            sb = jnp.where(mask, sb, NEG_INF)
        sk = sink_ref[head]
        m = jnp.maximum(jnp.max(sb, axis=1, keepdims=True), sk)
        es.append(jnp.exp((sb - m).astype(BF16)))
        sinks.append(jnp.exp(sk - m))
    o = _dot(jnp.concatenate(es, axis=0), _with_ones(v_cat))
    outs = []
    for j in range(tiles):
        halves = []
        for blk in (2 * j, 2 * j + 1):
            ob = o[blk * m_rows:(blk + 1) * m_rows]
            halves.append(ob[:, :LANES] / (ob[:, LANES:] + sinks[blk]))
        outs.append(jnp.where(lo, halves[0], halves[1]))
    return jnp.concatenate(outs, axis=1)


def _mha_pair(qt, k_cat, v_cat, bias_pair, n_biased, m_rows):
    lo = lax.broadcasted_iota(jnp.int32, (m_rows, LANES), 1) < HEAD_DIM
    s = _dot_nt(_split_heads(qt, lo), k_cat)
    if bias_pair is not None:
        s_w = s[:, :n_biased] + bias_pair
        s_c = s[:, n_biased:]
        m = jnp.maximum(jnp.max(s_w, axis=1, keepdims=True), jnp.max(s_c, axis=1, keepdims=True))
        e = jnp.concatenate([jnp.exp((s_w - m).astype(BF16)), jnp.exp((s_c - m).astype(BF16))], axis=1)
    else:
        e = jnp.exp((s - jnp.max(s, axis=1, keepdims=True)).astype(BF16))
    o = _dot(e, _with_ones(v_cat))
    o = o[:, :LANES] / o[:, LANES:]
    return jnp.where(lo, o[:m_rows], o[m_rows:])


def _short_conv(ub, prev_row, next_row, w_ref, m_rows):
    u = ub[:, :B_WIDTH].astype(F32)
    bb = ub[:, B_WIDTH:].astype(F32)
    row = lax.broadcasted_iota(jnp.int32, u.shape, 0)
    u_m1 = jnp.where(row == 0, prev_row, pltpu.roll(u, 1, 0))
    u_p1 = jnp.where(row == m_rows - 1, next_row, pltpu.roll(u, m_rows - 1, 0))
    y = u_m1 * w_ref[0:1, :] + u * w_ref[1:2, :] + u_p1 * w_ref[2:3, :]
    return bb * y


def _mix_latent_body(sink_ref, qa_ref, kp_ref, ko_ref, kn_ref, ubp_ref, ubo_ref, ubn_ref, qn_ref,
                     w0_ref, w1_ref, w2_ref, w3_ref, w4_ref, kvac_ref, kvnc_ref, bias_ref, cw_ref,
                     o_ref, *, seq):
    n = pl.program_id(1)
    nb = pl.num_programs(1)
    m_rows = A_BLOCK
    k_cat = jnp.concatenate([kp_ref[:, :A_KV_WIDTH], ko_ref[:, :A_KV_WIDTH], kn_ref[:, :A_KV_WIDTH],
                             kvac_ref[:, :A_KV_WIDTH]], axis=0)
    v_cat = jnp.concatenate([kp_ref[:, A_KV_WIDTH:], ko_ref[:, A_KV_WIDTH:], kn_ref[:, A_KV_WIDTH:],
                             kvac_ref[:, A_KV_WIDTH:]], axis=0)
    n_keys = k_cat.shape[0]
    qpos = n * A_BLOCK + lax.broadcasted_iota(jnp.int32, (m_rows, n_keys), 0)
    col = lax.broadcasted_iota(jnp.int32, (m_rows, n_keys), 1)
    kpos = (n - 1) * A_BLOCK + col
    mask = (col >= 3 * A_BLOCK) | ((jnp.abs(qpos - kpos) <= A_WINDOW) & (kpos >= 0) & (kpos < seq))
    o_a = _gqa(qa_ref[...], k_cat, v_cat, sink_ref, mask, m_rows)
    prev_row = jnp.where(n > 0, ubp_ref[HALO_ROWS - 1:HALO_ROWS, :B_WIDTH].astype(F32), 0.0)
    next_row = jnp.where(n < nb - 1, ubn_ref[0:1, :B_WIDTH].astype(F32), 0.0)
    o_b = _short_conv(ubo_ref[...], prev_row, next_row, cw_ref, m_rows)
    wins = (w0_ref, w1_ref, w2_ref, w3_ref, w4_ref)
    n_win = WIN_BLOCKS * A_BLOCK
    outs = []
    for j in range(C_WIDTH // LANES):
        ks = slice(j * LANES, (j + 1) * LANES)
        vs = slice(C_WIDTH + j * LANES, C_WIDTH + (j + 1) * LANES)
        kc = jnp.concatenate([w[:, ks] for w in wins] + [kvnc_ref[:, ks]], axis=0)
        vc = jnp.concatenate([w[:, vs] for w in wins] + [kvnc_ref[:, vs]], axis=0)
        bias_pair = jnp.concatenate([bias_ref[2 * j], bias_ref[2 * j + 1]], axis=0)
        outs.append(_mha_pair(qn_ref[:, ks], kc, vc, bias_pair, n_win, m_rows))
    o_c = jnp.concatenate(outs, axis=1)
    o_ref[...] = jnp.concatenate([o_a, o_b, o_c], axis=1).astype(BF16)


def _mix_latent(sink, qa, kva, ub, qn, kvn, kva_c, kvn_c, bias, conv_w):
    bsz, seq, _ = qa.shape
    nb = seq // A_BLOCK
    lctx = kva_c.shape[1]
    blk = A_BLOCK
    hb = blk // HALO_ROWS
    n_halo = seq // HALO_ROWS
    tok = lambda width, fn: pl.BlockSpec((None, blk, width), fn)
    own = lambda b, n: (b, n, 0)
    win = lambda i: (lambda b, n: (b, jnp.clip(n - 2, 0, nb - WIN_BLOCKS) + i, 0))
    pattern = lambda b, n: (jnp.minimum(n, 2) + jnp.maximum(n - (nb - 3), 0), 0, 0, 0)
    in_specs = [
        pl.BlockSpec(memory_space=pltpu.SMEM),
        tok(A_WIDTH, own),
        tok(2 * A_KV_WIDTH, lambda b, n: (b, jnp.maximum(n - 1, 0), 0)),
        tok(2 * A_KV_WIDTH, own),
        tok(2 * A_KV_WIDTH, lambda b, n: (b, jnp.minimum(n + 1, nb - 1), 0)),
        pl.BlockSpec((None, HALO_ROWS, 2 * B_WIDTH), lambda b, n: (b, jnp.maximum(n * hb - 1, 0), 0)),
        tok(2 * B_WIDTH, own),
        pl.BlockSpec((None, HALO_ROWS, 2 * B_WIDTH), lambda b, n: (b, jnp.minimum((n + 1) * hb, n_halo - 1), 0)),
        tok(C_WIDTH, own),
    ] + [tok(2 * C_WIDTH, win(i)) for i in range(WIN_BLOCKS)] + [
        pl.BlockSpec((None, lctx, 2 * A_KV_WIDTH), lambda b, n: (b, 0, 0)),
        pl.BlockSpec((None, lctx, 2 * C_WIDTH), lambda b, n: (b, 0, 0)),
        pl.BlockSpec((None, C_HEADS, blk, WIN_BLOCKS * blk), pattern),
        pl.BlockSpec(conv_w.shape, lambda b, n: (0, 0)),
    ]
    return pl.pallas_call(
        functools.partial(_mix_latent_body, seq=seq),
        grid=(bsz, nb),
        in_specs=in_specs,
        out_specs=tok(A_WIDTH + B_WIDTH + C_WIDTH, own),
        out_shape=jax.ShapeDtypeStruct((bsz, seq, A_WIDTH + B_WIDTH + C_WIDTH), BF16),
        compiler_params=_cparams(VMEM_LIMIT),
        name="mix_latent",
    )(sink, qa, kva, kva, kva, ub, ub, ub, qn, kvn, kvn, kvn, kvn, kvn, kva_c, kvn_c, bias, conv_w)


def _mix_ctx_body(sink_ref, qa_ref, kva_ref, ub_ref, qn_ref, kvn_ref, cw_ref, o_ref):
    m_rows = qa_ref.shape[0]
    o_a = _gqa(qa_ref[...], kva_ref[:, :A_KV_WIDTH], kva_ref[:, A_KV_WIDTH:], sink_ref, None, m_rows)
    o_b = _short_conv(ub_ref[...], 0.0, 0.0, cw_ref, m_rows)
    outs = []
    for j in range(C_WIDTH // LANES):
        ks = slice(j * LANES, (j + 1) * LANES)
        vs = slice(C_WIDTH + j * LANES, C_WIDTH + (j + 1) * LANES)
        outs.append(_mha_pair(qn_ref[:, ks], kvn_ref[:, ks], kvn_ref[:, vs], None, 0, m_rows))
    o_ref[...] = jnp.concatenate([o_a, o_b] + outs, axis=1).astype(BF16)


def _mix_ctx(sink, qa, kva, ub, qn, kvn, conv_w):
    bsz, lctx, _ = qa.shape
    full = lambda width: pl.BlockSpec((None, lctx, width), lambda b: (b, 0, 0))
    width = A_WIDTH + B_WIDTH + C_WIDTH
    return pl.pallas_call(
        _mix_ctx_body,
        grid=(bsz,),
        in_specs=[pl.BlockSpec(memory_space=pltpu.SMEM), full(A_WIDTH), full(2 * A_KV_WIDTH),
                  full(2 * B_WIDTH), full(C_WIDTH), full(2 * C_WIDTH),
                  pl.BlockSpec(conv_w.shape, lambda b: (0, 0))],
        out_specs=full(width),
        out_shape=jax.ShapeDtypeStruct((bsz, lctx, width), BF16),
        compiler_params=_cparams(VMEM_LIMIT),
        name="mix_ctx",
    )(sink, qa, kva, ub, qn, kvn, conv_w)


def _na_bias_tables(rpb, nb):
    per_blk = A_BLOCK // GRID_W
    rows = nb * per_blk
    kh = min(NA_ROWS, rows)
    qc = np.arange(GRID_W)[:, None]
    kc = np.arange(GRID_W)[None, :]
    coff = np.clip(kc - qc, -(NA_COLS - 1), NA_COLS - 1) + (NA_COLS - 1)
    cs = np.clip(qc - NA_COLS // 2, 0, GRID_W - NA_COLS)
    col_ok = (kc >= cs) & (kc < cs + NA_COLS)
    onehot = (coff[None] == np.arange(2 * NA_COLS - 1)[:, None, None]).astype(np.float32)
    tiles = jnp.einsum('hrc,cqk->hrqk', rpb, onehot, precision=lax.Precision.HIGHEST)
    tiles = jnp.where(col_ok, tiles, NEG_INF)
    masked = jnp.full((rpb.shape[0], GRID_W, GRID_W), NEG_INF, F32)
    patterns = []
    for blk in (0, 1, 2, nb - 2, nb - 1):
        wb = min(max(blk - 2, 0), nb - WIN_BLOCKS)
        q_rows = []
        for qi in range(per_blk):
            qr = per_blk * blk + qi
            rs = min(max(qr - kh // 2, 0), rows - kh)
            k_tiles = []
            for kj in range(WIN_BLOCKS * per_blk):
                kr = per_blk * wb + kj
                k_tiles.append(tiles[:, kr - qr + NA_ROWS - 1] if rs <= kr < rs + kh else masked)
            q_rows.append(jnp.concatenate(k_tiles, axis=-1))
        patterns.append(jnp.concatenate(q_rows, axis=-2))
    return jnp.stack(patterns)


def _router_affinity(h2b, wr_ref):
    logits = _dot(h2b, wr_ref[...])
    lane = lax.broadcasted_iota(jnp.int32, logits.shape, 1)
    logits = jnp.where(lane < N_EXPERTS, logits, NEG_INF)
    e = jnp.exp(logits - jnp.max(logits, axis=1, keepdims=True))
    return e / jnp.sum(e, axis=1, keepdims=True)


def _outproj_core(o_ref, x_ref, g1_ref, lg_ref, lb_ref, sh_ref, sc_ref, wo_ref, wr_ref, alpha):
    mix = _dot(o_ref[...], wo_ref[...])
    z = alpha * x_ref[...] + g1_ref[...] * mix
    x_mid = _ln(z) * lg_ref[...] + lb_ref[...]
    h2 = _ln(x_mid) * (1.0 + sc_ref[...]) + sh_ref[...]
    return x_mid, h2, _router_affinity(h2.astype(BF16), wr_ref)


def _outproj_body(o_ref, x_ref, g1_ref, lg_ref, lb_ref, sh_ref, sc_ref, wo_ref, wr_ref,
                  xm_ref, h2_ref, afft_ref, *, alpha, packed):
    x_mid, h2, aff = _outproj_core(o_ref, x_ref, g1_ref, lg_ref, lb_ref, sh_ref, sc_ref, wo_ref, wr_ref, alpha)
    xm_ref[...] = x_mid
    if packed:
        half = h2.shape[1] // 2
        h2_ref[...] = pltpu.pack_elementwise([h2[:, :half], h2[:, half:]], packed_dtype=BF16)
    else:
        h2_ref[...] = h2.astype(BF16)
    afft_ref[...] = aff.T[:N_EXPERTS, :]


def _outproj(o, x, g1, lg, lb, sh, sc, wo, wr, tm, alpha, latent):
    bsz, seq, d = x.shape
    tok = lambda width: pl.BlockSpec((None, tm, width), lambda b, i: (b, i, 0))
    vec = pl.BlockSpec((None, 1, d), lambda b, i: (b, 0, 0))
    par = pl.BlockSpec((1, d), lambda b, i: (0, 0))
    afft_spec = pl.BlockSpec((None, N_EXPERTS, tm), lambda b, i: (b, 0, i))
    afft_shape = jax.ShapeDtypeStruct((bsz, N_EXPERTS, seq), F32)
    h2_width, h2_dtype = (d // 2, jnp.uint32) if latent else (d, BF16)
    out_specs = [tok(d), tok(h2_width), afft_spec]
    out_shape = [jax.ShapeDtypeStruct((bsz, seq, d), F32), jax.ShapeDtypeStruct((bsz, seq, h2_width), h2_dtype),
                 afft_shape]
    return pl.pallas_call(
        functools.partial(_outproj_body, alpha=alpha, packed=latent),
        grid=(bsz, seq // tm),
        in_specs=[tok(d), tok(d), vec, par, par, vec, vec,
                  pl.BlockSpec(wo.shape, lambda b, i: (0, 0)), pl.BlockSpec(wr.shape, lambda b, i: (0, 0))],
        out_specs=out_specs,
        out_shape=out_shape,
        compiler_params=_cparams(VMEM_LIMIT),
        name="outproj_latent" if latent else "outproj_ctx",
    )(o, x, g1, lg, lb, sh, sc, wo, wr)


def _kth_largest_bucket(aff, cap, axes):
    shape = list(aff.shape)
    for a in axes:
        shape[a] = 1
    thr = jnp.zeros(shape, jnp.int32)
    for bit in range(30, -1, -1):
        cand = thr | jnp.int32(1 << bit)
        cnt = _count(aff >= lax.bitcast_convert_type(cand, F32), axes)
        thr = jnp.where(cnt >= cap, cand, thr)
    return lax.bitcast_convert_type(thr, F32), lax.bitcast_convert_type(thr + 1, F32)


def _count(mask, axes):
    c = jnp.where(mask, 1.0, 0.0)
    for a in sorted(axes, reverse=True):
        c = jnp.sum(c, axis=a, keepdims=True)
    return c


def _select_latent_body(a_ref, u_ref, lt_ref, lb_ref, idx_ref, gate_ref, *, cap, nc):
    aff = a_ref[...]
    rows = aff.shape[0]
    aff3 = aff.reshape(N_EXPERTS, nc, LANES)
    t_lo, t_hi = _kth_largest_bucket(aff3, cap, (1, 2))
    gt = aff3 >= t_hi
    eq = (aff3 >= t_lo) & (aff3 < t_hi)
    need = cap - _count(gt, (1, 2))
    eq_f = jnp.where(eq, 1.0, 0.0).reshape(rows, LANES)
    eq_b = eq_f.astype(BF16)
    rank = (_dot(eq_b, u_ref[...]) - eq_f
            + jnp.sum(_dot(lb_ref[...], eq_b), axis=1, keepdims=True))
    sel3 = gt | (eq & (rank.reshape(N_EXPERTS, nc, LANES) < need))
    sel_f = jnp.where(sel3, 1.0, 0.0).reshape(rows, LANES)
    sel_b = sel_f.astype(BF16)
    cnt = jnp.sum(sel_f, axis=1, keepdims=True)
    g_incl = jnp.sum(_dot(lb_ref[...], sel_b), axis=1, keepdims=True) + cnt
    r_row = lax.broadcasted_iota(jnp.int32, (1, cap), 1).astype(F32)
    c_col = lax.broadcasted_iota(jnp.int32, (nc, 1), 0).astype(F32)
    l_col = lax.broadcasted_iota(jnp.int32, (LANES, 1), 0).astype(F32)
    for e in range(N_EXPERTS):
        sl = slice(e * nc, (e + 1) * nc)
        before = g_incl[sl] <= r_row
        chunk = jnp.sum(jnp.where(before, 1.0, 0.0), axis=0, keepdims=True)
        base = jnp.sum(jnp.where(before, cnt[sl], 0.0), axis=0, keepdims=True)
        onehot = jnp.where(c_col == chunk, 1.0, 0.0)
        cs_t = _dot_nt(lt_ref[...], sel_b[sl]).astype(BF16)
        cs_of_r = _dot(cs_t, onehot.astype(BF16))
        local = jnp.sum(jnp.where(cs_of_r <= r_row - base, 1.0, 0.0), axis=0, keepdims=True)
        idx_ref[e:e + 1, :] = (chunk * LANES + local).astype(jnp.int32)
        aff_of_r = jnp.dot(aff[sl].T, onehot, precision=lax.Precision.HIGHEST, preferred_element_type=F32)
        gate_ref[e:e + 1, :] = jnp.sum(jnp.where(l_col == local, aff_of_r, 0.0), axis=0, keepdims=True)


def _select_latent(aff_t, cap):
    bsz, n_exp, seq = aff_t.shape
    nc = seq // LANES
    rows = n_exp * nc
    tri = np.triu(np.ones((LANES, LANES), np.float32))
    r = np.arange(rows)
    blockdiag = ((r[:, None] // nc == r[None, :] // nc) & (r[None, :] < r[:, None])).astype(np.float32)
    const = lambda shape: pl.BlockSpec(shape, lambda b: (0, 0))
    return pl.pallas_call(
        functools.partial(_select_latent_body, cap=cap, nc=nc),
        grid=(bsz,),
        in_specs=[pl.BlockSpec((None, rows, LANES), lambda b: (b, 0, 0)),
                  const((LANES, LANES)), const((LANES, LANES)), const((rows, rows))],
        out_specs=[pl.BlockSpec((None, n_exp, cap), lambda b: (b, 0, 0))] * 2,
        out_shape=[jax.ShapeDtypeStruct((bsz, n_exp, cap), jnp.int32),
                   jax.ShapeDtypeStruct((bsz, n_exp, cap), F32)],
        compiler_params=_cparams(VMEM_LIMIT),
        name="select_latent",
    )(aff_t.reshape(bsz, rows, LANES), jnp.asarray(tri, BF16), jnp.asarray(tri.T, BF16),
      jnp.asarray(blockdiag, BF16))


def _select_ctx_body(a_ref, u_ref, w_ref, *, cap):
    aff = a_ref[...]
    t_lo, t_hi = _kth_largest_bucket(aff, cap, (1,))
    gt = aff >= t_hi
    eq = (aff >= t_lo) & (aff < t_hi)
    need = cap - _count(gt, (1,))
    eq_f = jnp.where(eq, 1.0, 0.0)
    rank = _dot(eq_f.astype(BF16), u_ref[...]) - eq_f
    w_ref[...] = jnp.where(gt | (eq & (rank < need)), aff, 0.0)


def _select_ctx(aff_t, cap):
    bsz, n_exp, lctx = aff_t.shape
    tri = np.triu(np.ones((lctx, lctx), np.float32))
    blk = pl.BlockSpec((None, n_exp, lctx), lambda b: (b, 0, 0))
    return pl.pallas_call(
        functools.partial(_select_ctx_body, cap=cap),
        grid=(bsz,),
        in_specs=[blk, pl.BlockSpec((lctx, lctx), lambda b: (0, 0))],
        out_specs=blk,
        out_shape=jax.ShapeDtypeStruct((bsz, n_exp, lctx), F32),
        name="select_ctx",
    )(aff_t, jnp.asarray(tri, BF16))


def _swiglu(xe, wg_ref, wu_ref, wd_ref):
    a = _dot(xe, wg_ref[...].astype(BF16))
    u = _dot(xe, wu_ref[...].astype(BF16))
    act = (a * jax.nn.sigmoid(a) * u).astype(BF16)
    return _dot(act, wd_ref[...].astype(BF16))


def _expert_gather_body(idx_ref, hp_ref, wg_ref, wu_ref, wd_ref, y_ref, xg_ref, *, cap):
    def gather_group(g, carry):
        for k in range(SUBLANES):
            xg_ref[g, k:k + 1, :] = hp_ref[pl.ds(idx_ref[0, 0, g * SUBLANES + k], 1), :]
        return carry

    lax.fori_loop(0, cap // SUBLANES, gather_group, 0)
    packed = xg_ref[...].reshape(cap, xg_ref.shape[-1])
    halves = [pltpu.unpack_elementwise(packed, index=i, packed_dtype=BF16, unpacked_dtype=F32).astype(BF16)
              for i in range(2)]
    y_ref[...] = _swiglu(jnp.concatenate(halves, axis=1), wg_ref, wu_ref, wd_ref)


def _expert_gather(idx, hp, wg, wu, wd, layer):
    bsz, n_exp, cap = idx.shape
    seq, width = hp.shape[1:]
    d, ff = wg.shape[2:]
    return pl.pallas_call(
        functools.partial(_expert_gather_body, cap=cap),
        grid=(bsz, n_exp),
        in_specs=[
            pl.BlockSpec((1, 1, cap), lambda b, e: (b * n_exp + e, 0, 0), memory_space=pltpu.SMEM),
            pl.BlockSpec((None, seq, width), lambda b, e: (b, 0, 0), pipeline_mode=pl.Buffered(1)),
            pl.BlockSpec((None, None, d, ff), lambda b, e: (layer, e, 0, 0)),
            pl.BlockSpec((None, None, d, ff), lambda b, e: (layer, e, 0, 0)),
            pl.BlockSpec((None, None, ff, d), lambda b, e: (layer, e, 0, 0)),
        ],
        out_specs=pl.BlockSpec((None, None, cap, d), lambda b, e: (b, e, 0, 0)),
        out_shape=jax.ShapeDtypeStruct((bsz, n_exp, cap, d), F32),
        scratch_shapes=[pltpu.VMEM((cap // SUBLANES, SUBLANES, width), jnp.uint32)],
        compiler_params=_cparams(VMEM_LIMIT),
        name="expert_gather",
    )(idx.reshape(bsz * n_exp, 1, cap), hp, wg, wu, wd)


def _scatter_body(idx_ref, gate_ref, y_ref, o_ref, *, cap):
    @pl.when(pl.program_id(1) == 0)
    def _():
        o_ref[...] = jnp.zeros_like(o_ref)

    def add_group(g, carry):
        r0 = g * SUBLANES
        ts = [idx_ref[0, 0, r0 + k] for k in range(SUBLANES)]
        acc = [o_ref[pl.ds(ts[k], 1), :] + gate_ref[0, 0, r0 + k] * y_ref[g, k:k + 1, :]
               for k in range(SUBLANES)]
        for k in range(SUBLANES):
            o_ref[pl.ds(ts[k], 1), :] = acc[k]
        return carry

    lax.fori_loop(0, cap // SUBLANES, add_group, 0)


def _scatter_add(idx, gate, y, seq):
    bsz, n_exp, cap, d = y.shape
    sel = lambda b, e: (b * n_exp + e, 0, 0)
    return pl.pallas_call(
        functools.partial(_scatter_body, cap=cap),
        grid=(bsz, n_exp),
        in_specs=[
            pl.BlockSpec((1, 1, cap), sel, memory_space=pltpu.SMEM),
            pl.BlockSpec((1, 1, cap), sel, memory_space=pltpu.SMEM),
            pl.BlockSpec((None, None, cap // SUBLANES, SUBLANES, d), lambda b, e: (b, e, 0, 0, 0)),
        ],
        out_specs=pl.BlockSpec((None, seq, d), lambda b, e: (b, 0, 0), pipeline_mode=pl.Buffered(1)),
        out_shape=jax.ShapeDtypeStruct((bsz, seq, d), F32),
        compiler_params=_cparams(VMEM_LIMIT),
        name="scatter_add",
    )(idx.reshape(bsz * n_exp, 1, cap), gate.reshape(bsz * n_exp, 1, cap),
      y.reshape(bsz, n_exp, cap // SUBLANES, SUBLANES, d))


def _expert_dense_body(h_ref, w_ref, wg_ref, wu_ref, wd_ref, o_ref):
    e = pl.program_id(0)

    @pl.when(e == 0)
    def _():
        o_ref[...] = jnp.zeros_like(o_ref)

    w = w_ref[...]
    lane = lax.broadcasted_iota(jnp.int32, w.shape, 1)
    gate = jnp.sum(jnp.where(lane == e, w, 0.0), axis=1, keepdims=True)
    o_ref[...] += _swiglu(h_ref[...], wg_ref, wu_ref, wd_ref) * gate


def _expert_dense(h, w, wg, wu, wd, layer):
    t, d = h.shape
    n_exp, _, ff = wg.shape[1:]
    return pl.pallas_call(
        _expert_dense_body,
        grid=(n_exp,),
        in_specs=[
            pl.BlockSpec((t, d), lambda e: (0, 0)),
            pl.BlockSpec(w.shape, lambda e: (0, 0)),
            pl.BlockSpec((None, None, d, ff), lambda e: (layer, e, 0, 0)),
            pl.BlockSpec((None, None, d, ff), lambda e: (layer, e, 0, 0)),
            pl.BlockSpec((None, None, ff, d), lambda e: (layer, e, 0, 0)),
        ],
        out_specs=pl.BlockSpec((t, d), lambda e: (0, 0)),
        out_shape=jax.ShapeDtypeStruct((t, d), F32),
        compiler_params=_cparams(VMEM_LIMIT),
        name="expert_dense",
    )(h, w, wg, wu, wd)


def _final_ln_body(xm_ref, f_ref, g2_ref, lg_ref, lb_ref, o_ref, *, alpha):
    z = alpha * xm_ref[...] + g2_ref[...] * f_ref[...]
    o_ref[...] = _ln(z) * lg_ref[...] + lb_ref[...]


def _final_ln(xm, ffn, g2, lg, lb, tm, alpha):
    bsz, seq, d = xm.shape
    tok = pl.BlockSpec((None, tm, d), lambda b, i: (b, i, 0))
    par = pl.BlockSpec((1, d), lambda b, i: (0, 0))
    return pl.pallas_call(
        functools.partial(_final_ln_body, alpha=alpha),
        grid=(bsz, seq // tm),
        in_specs=[tok, tok, pl.BlockSpec((None, 1, d), lambda b, i: (b, 0, 0)), par, par],
        out_specs=tok,
        out_shape=jax.ShapeDtypeStruct((bsz, seq, d), F32),
        compiler_params=_cparams(VMEM_LIMIT),
        name="final_ln",
    )(xm, ffn, g2, lg, lb)


def _rope_tables(seq):
    t = np.arange(seq)
    inv = ROPE_THETA ** (-np.arange(0, ROPE_AXIS_DIM, 2, dtype=np.float32) / ROPE_AXIS_DIM)
    inv = jnp.asarray(inv, F32)

    def axis(pos):
        ang = jnp.asarray(pos, F32)[:, None] * inv[None, :]
        c, s = jnp.cos(ang), jnp.sin(ang)
        return jnp.concatenate([c, c], axis=1), jnp.concatenate([-s, s], axis=1)

    cr, sr = axis(t // GRID_W)
    cc, sc = axis(t % GRID_W)
    reps = LANES // HEAD_DIM
    return (jnp.tile(jnp.concatenate([cr, cc], axis=1), (1, reps)),
            jnp.tile(jnp.concatenate([sr, sc], axis=1), (1, reps)))


def _token_tile(seq):
    return 512 if seq % 512 == 0 else seq


def kernel(x, c, ctx, c_ctx, w_mod, b_mod, w_in, conv_w, attn_sink, na_rpb, w_out, ln1_g, ln1_b,
           w_router, w_gate, w_up, w_down, ln2_g, ln2_b):
    bsz, seq, d = x.shape
    lctx = ctx.shape[1]
    depth = w_mod.shape[0]
    alpha = (2 * depth) ** 0.25
    nb = seq // A_BLOCK
    assert seq % A_BLOCK == 0 and nb >= WIN_BLOCKS + 1 and bsz + 1 <= MOD_ROWS
    tm = _token_tile(seq)
    tmc = _token_tile(bsz * lctx)

    cond = jnp.zeros((MOD_ROWS, d), F32).at[:bsz].set(c).at[bsz].set(c_ctx)
    mod = _modulation(cond, w_mod, b_mod)

    cos, sin = _rope_tables(seq)
    cos_c = jnp.ones((bsz * lctx, LANES), F32)
    sin_c = jnp.zeros((bsz * lctx, LANES), F32)

    head_order = [t * A_GROUP + j for j in range(A_GROUP) for t in range(A_KV_HEADS)]
    a_cols = np.concatenate([np.arange(h * HEAD_DIM, (h + 1) * HEAD_DIM) for h in head_order])
    in_cols = np.concatenate([a_cols, np.arange(A_WIDTH, IN_WIDTH)])
    out_rows = np.concatenate([a_cols, np.arange(A_WIDTH, w_out.shape[1])])

    cap = CAPACITY_FACTOR * seq // N_EXPERTS
    cap_c = CAPACITY_FACTOR * lctx // N_EXPERTS
    ctx_flat = ctx.reshape(1, bsz * lctx, d)

    for l in range(depth):
        last = l == depth - 1
        m6 = mod[l].reshape(MOD_ROWS, N_MOD, d)
        sh1, sc1, g1, sh2, sc2, g2 = (m6[:bsz, i].reshape(bsz, 1, d) for i in range(N_MOD))
        csh1, csc1, cg1, csh2, csc2, cg2 = (m6[bsz:bsz + 1, i].reshape(1, 1, d) for i in range(N_MOD))
        w_in_l = w_in[l][:, in_cols].astype(BF16)
        w_out_l = w_out[l][out_rows, :].astype(BF16)
        w_r = jnp.zeros((d, LANES), BF16).at[:, :N_EXPERTS].set(w_router[l].astype(BF16))
        sink = attn_sink[l]
        lg1, lb1 = ln1_g[l].reshape(1, d), ln1_b[l].reshape(1, d)
        lg2, lb2 = ln2_g[l].reshape(1, d), ln2_b[l].reshape(1, d)

        qa, kva, ub, qn, kvn = _inproj(x, sh1, sc1, cos, sin, w_in_l, tm)
        qa_c, kva_c, ub_c, qn_c, kvn_c = (
            a.reshape(bsz, lctx, a.shape[-1])
            for a in _inproj(ctx_flat, csh1, csc1, cos_c, sin_c, w_in_l, tmc))
        bias = _na_bias_tables(na_rpb[l], nb)
        o = _mix_latent(sink, qa, kva, ub, qn, kvn, kva_c, kvn_c, bias, conv_w[l])
        x_mid, hp, aff_t = _outproj(o, x, g1, lg1, lb1, sh2, sc2, w_out_l, w_r, tm, alpha, True)

        if not last:
            o_c = _mix_ctx(sink, qa_c, kva_c, ub_c, qn_c, kvn_c, conv_w[l])
            ctx_mid, h2c, aff_ct = _outproj(
                o_c.reshape(1, bsz * lctx, d), ctx_flat, cg1, lg1, lb1, csh2, csc2, w_out_l, w_r,
                tmc, alpha, False)
            aff_ct = jnp.transpose(aff_ct.reshape(N_EXPERTS, bsz, lctx), (1, 0, 2))
            gate_c = _select_ctx(aff_ct, cap_c)
            gate_c = jnp.transpose(gate_c, (0, 2, 1)).reshape(bsz * lctx, N_EXPERTS)
            ffn_c = _expert_dense(h2c[0], gate_c, w_gate, w_up, w_down, l)
            ctx_flat = _final_ln(ctx_mid, ffn_c[None], cg2, lg2, lb2, tmc, alpha)

        idx, gate = _select_latent(aff_t, cap)
        y = _expert_gather(idx, hp, w_gate, w_up, w_down, l)
        ffn = _scatter_add(idx, gate, y, seq)
        x = _final_ln(x_mid, ffn, g2, lg2, lb2, tm, alpha)
    return x
```

```python
import functools

import numpy as np
import jax
import jax.numpy as jnp
from jax import lax
from jax.experimental import pallas as pl
from jax.experimental.pallas import tpu as pltpu

HEAD_DIM = 64
GRID_W = 64
A_Q_HEADS = 6
A_KV_HEADS = 2
A_GROUP = A_Q_HEADS // A_KV_HEADS
A_WINDOW = 128
A_BLOCK = 128
B_WIDTH = 256
C_HEADS = 6
NA_ROWS = 8
NA_COLS = 16
A_WIDTH = A_Q_HEADS * HEAD_DIM
A_KV_WIDTH = A_KV_HEADS * HEAD_DIM
C_WIDTH = C_HEADS * HEAD_DIM
N_EXPERTS = 16
CAPACITY_FACTOR = 2
ROPE_THETA = 10000.0
ROPE_AXIS_DIM = HEAD_DIM // 2
LN_EPS = 1e-6
N_MOD = 6
NEG_INF = -1e30

LANES = 128
SUBLANES = 8
MXU_COLS = 256
MOD_ROWS = SUBLANES
HALO_ROWS = 16
WIN_BLOCKS = 5
N_BIAS_PATTERNS = 5
VMEM_LIMIT = 56 * 1024 * 1024

BF16 = jnp.bfloat16
F32 = jnp.float32


def _cparams(vmem=None):
    return pltpu.CompilerParams(vmem_limit_bytes=vmem) if vmem else None


def _dot(a, b):
    return jnp.dot(a, b, preferred_element_type=F32)


def _dot_nt(a, b):
    return lax.dot_general(a, b, (((1,), (1,)), ((), ())), preferred_element_type=F32)


def _ln(x):
    mu = jnp.mean(x, axis=-1, keepdims=True)
    xc = x - mu
    var = jnp.mean(xc * xc, axis=-1, keepdims=True)
    return xc * lax.rsqrt(var + LN_EPS)


def _mod_body(c_ref, w_ref, b_ref, o_ref):
    c = c_ref[...]
    h = (c * jax.nn.sigmoid(c)).astype(BF16)
    o_ref[...] = _dot(h, w_ref[...].astype(BF16)) + b_ref[...]


def _modulation(cond, w_mod, b_mod):
    depth, d, n = w_mod.shape
    tn = n // 4
    return pl.pallas_call(
        _mod_body,
        grid=(depth, n // tn),
        in_specs=[
            pl.BlockSpec((MOD_ROWS, d), lambda l, j: (0, 0)),
            pl.BlockSpec((None, d, tn), lambda l, j: (l, 0, j)),
            pl.BlockSpec((None, 1, tn), lambda l, j: (l, 0, j)),
        ],
        out_specs=pl.BlockSpec((None, MOD_ROWS, tn), lambda l, j: (l, 0, j)),
        out_shape=jax.ShapeDtypeStruct((depth, MOD_ROWS, n), F32),
        compiler_params=_cparams(VMEM_LIMIT),
        name="modulation",
    )(cond, w_mod, b_mod.reshape(depth, 1, n))


QK_W = A_WIDTH + HEAD_DIM * A_KV_HEADS
O_VA = QK_W
O_B = O_VA + A_KV_WIDTH
O_QN = O_B + 3 * B_WIDTH
O_KVN = O_QN + C_WIDTH
IN_WIDTH = O_KVN + 2 * C_WIDTH


def _inproj_body(x_ref, sh_ref, sc_ref, cos_ref, sin_ref, w_ref,
                 qa_ref, kva_ref, ub_ref, qn_ref, kvn_ref):
    h = _ln(x_ref[...]) * (1.0 + sc_ref[...]) + sh_ref[...]
    hb = h.astype(BF16)
    scale = HEAD_DIM ** -0.5
    qk = _dot(hb, w_ref[:, 0:QK_W])
    cos = cos_ref[...]
    sin = sin_ref[...]
    lane = lax.broadcasted_iota(jnp.int32, cos.shape, 1)
    first = (lane % ROPE_AXIS_DIM) < (ROPE_AXIS_DIM // 2)
    half = ROPE_AXIS_DIM // 2
    parts = []
    for j in range(QK_W // LANES):
        t = qk[:, j * LANES:(j + 1) * LANES]
        partner = jnp.where(first, pltpu.roll(t, LANES - half, 1), pltpu.roll(t, half, 1))
        parts.append(t * cos + partner * sin)
    qa_ref[...] = (jnp.concatenate(parts[:-1], axis=1) * scale).astype(BF16)
    va = _dot(hb, w_ref[:, O_VA:O_B])
    kva_ref[...] = jnp.concatenate([parts[-1], va], axis=1).astype(BF16)
    g = _dot(hb, w_ref[:, O_B:O_QN])
    bx, bb, bc = g[:, :B_WIDTH], g[:, B_WIDTH:2 * B_WIDTH], g[:, 2 * B_WIDTH:]
    ub_ref[...] = jnp.concatenate([bc * bx, bb], axis=1).astype(BF16)
    qn_ref[...] = (_dot(hb, w_ref[:, O_QN:O_KVN]) * scale).astype(BF16)
    kvn_ref[...] = _dot(hb, w_ref[:, O_KVN:IN_WIDTH]).astype(BF16)


def _inproj(x, sh, sc, cos, sin, w, tm):
    bsz, seq, d = x.shape
    tok = lambda width: pl.BlockSpec((None, tm, width), lambda b, i: (b, i, 0))
    vec = pl.BlockSpec((None, 1, d), lambda b, i: (b, 0, 0))
    tab = pl.BlockSpec((tm, LANES), lambda b, i: (i, 0))
    widths = (A_WIDTH, 2 * A_KV_WIDTH, 2 * B_WIDTH, C_WIDTH, 2 * C_WIDTH)
    return pl.pallas_call(
        _inproj_body,
        grid=(bsz, seq // tm),
        in_specs=[tok(d), vec, vec, tab, tab, pl.BlockSpec((d, IN_WIDTH), lambda b, i: (0, 0))],
        out_specs=[tok(wd) for wd in widths],
        out_shape=[jax.ShapeDtypeStruct((bsz, seq, wd), BF16) for wd in widths],
        compiler_params=_cparams(VMEM_LIMIT),
        name="inproj",
    )(x, sh, sc, cos, sin, w)


def _split_heads(q, lo):
    zero = jnp.zeros_like(q)
    return jnp.concatenate([jnp.where(lo, q, zero), jnp.where(lo, zero, q)], axis=0)


def _with_ones(v_cat):
    return jnp.concatenate([v_cat, jnp.ones_like(v_cat)], axis=1)


def _gqa(qa, k_cat, v_cat, sink_ref, mask, m_rows):
    lo = lax.broadcasted_iota(jnp.int32, (m_rows, LANES), 1) < HEAD_DIM
    tiles = A_WIDTH // LANES
    qs = jnp.concatenate([_split_heads(qa[:, j * LANES:(j + 1) * LANES], lo) for j in range(tiles)], axis=0)
    s = _dot_nt(qs, k_cat)
    es, sinks = [], []
    for blk in range(2 * tiles):
        head = (blk % 2) * A_GROUP + blk // 2
        sb = s[blk * m_rows:(blk + 1) * m_rows]
        if mask is not None:
            sb = jnp.where(mask, sb, NEG_INF)
        sk = sink_ref[head]
        m = jnp.maximum(jnp.max(sb, axis=1, keepdims=True), sk)
        es.append(jnp.exp((sb - m).astype(BF16)))
        sinks.append(jnp.exp(sk - m))
    o = _dot(jnp.concatenate(es, axis=0), _with_ones(v_cat))
    outs = []
    for j in range(tiles):
        halves = []
        for blk in (2 * j, 2 * j + 1):
            ob = o[blk * m_rows:(blk + 1) * m_rows]
            halves.append(ob[:, :LANES] / (ob[:, LANES:] + sinks[blk]))
        outs.append(jnp.where(lo, halves[0], halves[1]))
    return jnp.concatenate(outs, axis=1)


def _mha_pair(qt, k_cat, v_cat, bias_pair, n_biased, m_rows):
    lo = lax.broadcasted_iota(jnp.int32, (m_rows, LANES), 1) < HEAD_DIM
    s = _dot_nt(_split_heads(qt, lo), k_cat)
    if bias_pair is not None:
        s_w = s[:, :n_biased] + bias_pair
        s_c = s[:, n_biased:]
        m = jnp.maximum(jnp.max(s_w, axis=1, keepdims=True), jnp.max(s_c, axis=1, keepdims=True))
        e = jnp.concatenate([jnp.exp((s_w - m).astype(BF16)), jnp.exp((s_c - m).astype(BF16))], axis=1)
    else:
        e = jnp.exp((s - jnp.max(s, axis=1, keepdims=True)).astype(BF16))
    o = _dot(e, _with_ones(v_cat))
    o = o[:, :LANES] / o[:, LANES:]
    return jnp.where(lo, o[:m_rows], o[m_rows:])


def _short_conv(ub, prev_row, next_row, w_ref, m_rows):
    u = ub[:, :B_WIDTH].astype(F32)
    bb = ub[:, B_WIDTH:].astype(F32)
    row = lax.broadcasted_iota(jnp.int32, u.shape, 0)
    u_m1 = jnp.where(row == 0, prev_row, pltpu.roll(u, 1, 0))
    u_p1 = jnp.where(row == m_rows - 1, next_row, pltpu.roll(u, m_rows - 1, 0))
    y = u_m1 * w_ref[0:1, :] + u * w_ref[1:2, :] + u_p1 * w_ref[2:3, :]
    return bb * y


def _mix_latent_body(sink_ref, qa_ref, kp_ref, ko_ref, kn_ref, ubp_ref, ubo_ref, ubn_ref, qn_ref,
                     w0_ref, w1_ref, w2_ref, w3_ref, w4_ref, kvac_ref, kvnc_ref, bias_ref, cw_ref,
                     o_ref, *, seq):
    n = pl.program_id(1)
    nb = pl.num_programs(1)
    m_rows = A_BLOCK
    k_cat = jnp.concatenate([kp_ref[:, :A_KV_WIDTH], ko_ref[:, :A_KV_WIDTH], kn_ref[:, :A_KV_WIDTH],
                             kvac_ref[:, :A_KV_WIDTH]], axis=0)
    v_cat = jnp.concatenate([kp_ref[:, A_KV_WIDTH:], ko_ref[:, A_KV_WIDTH:], kn_ref[:, A_KV_WIDTH:],
                             kvac_ref[:, A_KV_WIDTH:]], axis=0)
    n_keys = k_cat.shape[0]
    qpos = n * A_BLOCK + lax.broadcasted_iota(jnp.int32, (m_rows, n_keys), 0)
    col = lax.broadcasted_iota(jnp.int32, (m_rows, n_keys), 1)
    kpos = (n - 1) * A_BLOCK + col
    mask = (col >= 3 * A_BLOCK) | ((jnp.abs(qpos - kpos) <= A_WINDOW) & (kpos >= 0) & (kpos < seq))
    o_a = _gqa(qa_ref[...], k_cat, v_cat, sink_ref, mask, m_rows)
    prev_row = jnp.where(n > 0, ubp_ref[HALO_ROWS - 1:HALO_ROWS, :B_WIDTH].astype(F32), 0.0)
    next_row = jnp.where(n < nb - 1, ubn_ref[0:1, :B_WIDTH].astype(F32), 0.0)
    o_b = _short_conv(ubo_ref[...], prev_row, next_row, cw_ref, m_rows)
    wins = (w0_ref, w1_ref, w2_ref, w3_ref, w4_ref)
    n_win = WIN_BLOCKS * A_BLOCK
    outs = []
    for j in range(C_WIDTH // LANES):
        ks = slice(j * LANES, (j + 1) * LANES)
        vs = slice(C_WIDTH + j * LANES, C_WIDTH + (j + 1) * LANES)
        kc = jnp.concatenate([w[:, ks] for w in wins] + [kvnc_ref[:, ks]], axis=0)
        vc = jnp.concatenate([w[:, vs] for w in wins] + [kvnc_ref[:, vs]], axis=0)
        bias_pair = jnp.concatenate([bias_ref[2 * j], bias_ref[2 * j + 1]], axis=0)
        outs.append(_mha_pair(qn_ref[:, ks], kc, vc, bias_pair, n_win, m_rows))
    o_c = jnp.concatenate(outs, axis=1)
    o_ref[...] = jnp.concatenate([o_a, o_b, o_c], axis=1).astype(BF16)


def _mix_latent(sink, qa, kva, ub, qn, kvn, kva_c, kvn_c, bias, conv_w):
    bsz, seq, _ = qa.shape
    nb = seq // A_BLOCK
    lctx = kva_c.shape[1]
    blk = A_BLOCK
    hb = blk // HALO_ROWS
    n_halo = seq // HALO_ROWS
    tok = lambda width, fn: pl.BlockSpec((None, blk, width), fn)
    own = lambda b, n: (b, n, 0)
    win = lambda i: (lambda b, n: (b, jnp.clip(n - 2, 0, nb - WIN_BLOCKS) + i, 0))
    pattern = lambda b, n: (jnp.minimum(n, 2) + jnp.maximum(n - (nb - 3), 0), 0, 0, 0)
    in_specs = [
        pl.BlockSpec(memory_space=pltpu.SMEM),
        tok(A_WIDTH, own),
        tok(2 * A_KV_WIDTH, lambda b, n: (b, jnp.maximum(n - 1, 0), 0)),
        tok(2 * A_KV_WIDTH, own),
        tok(2 * A_KV_WIDTH, lambda b, n: (b, jnp.minimum(n + 1, nb - 1), 0)),
        pl.BlockSpec((None, HALO_ROWS, 2 * B_WIDTH), lambda b, n: (b, jnp.maximum(n * hb - 1, 0), 0)),
        tok(2 * B_WIDTH, own),
        pl.BlockSpec((None, HALO_ROWS, 2 * B_WIDTH), lambda b, n: (b, jnp.minimum((n + 1) * hb, n_halo - 1), 0)),
        tok(C_WIDTH, own),
    ] + [tok(2 * C_WIDTH, win(i)) for i in range(WIN_BLOCKS)] + [
        pl.BlockSpec((None, lctx, 2 * A_KV_WIDTH), lambda b, n: (b, 0, 0)),
        pl.BlockSpec((None, lctx, 2 * C_WIDTH), lambda b, n: (b, 0, 0)),
        pl.BlockSpec((None, C_HEADS, blk, WIN_BLOCKS * blk), pattern),
        pl.BlockSpec(conv_w.shape, lambda b, n: (0, 0)),
    ]
    return pl.pallas_call(
        functools.partial(_mix_latent_body, seq=seq),
        grid=(bsz, nb),
        in_specs=in_specs,
        out_specs=tok(A_WIDTH + B_WIDTH + C_WIDTH, own),
        out_shape=jax.ShapeDtypeStruct((bsz, seq, A_WIDTH + B_WIDTH + C_WIDTH), BF16),
        compiler_params=_cparams(VMEM_LIMIT),
        name="mix_latent",
    )(sink, qa, kva, kva, kva, ub, ub, ub, qn, kvn, kvn, kvn, kvn, kvn, kva_c, kvn_c, bias, conv_w)


def _mix_ctx_body(sink_ref, qa_ref, kva_ref, ub_ref, qn_ref, kvn_ref, cw_ref, o_ref):
    m_rows = qa_ref.shape[0]
    o_a = _gqa(qa_ref[...], kva_ref[:, :A_KV_WIDTH], kva_ref[:, A_KV_WIDTH:], sink_ref, None, m_rows)
    o_b = _short_conv(ub_ref[...], 0.0, 0.0, cw_ref, m_rows)
    outs = []
    for j in range(C_WIDTH // LANES):
        ks = slice(j * LANES, (j + 1) * LANES)
        vs = slice(C_WIDTH + j * LANES, C_WIDTH + (j + 1) * LANES)
        outs.append(_mha_pair(qn_ref[:, ks], kvn_ref[:, ks], kvn_ref[:, vs], None, 0, m_rows))
    o_ref[...] = jnp.concatenate([o_a, o_b] + outs, axis=1).astype(BF16)


def _mix_ctx(sink, qa, kva, ub, qn, kvn, conv_w):
    bsz, lctx, _ = qa.shape
    full = lambda width: pl.BlockSpec((None, lctx, width), lambda b: (b, 0, 0))
    width = A_WIDTH + B_WIDTH + C_WIDTH
    return pl.pallas_call(
        _mix_ctx_body,
        grid=(bsz,),
        in_specs=[pl.BlockSpec(memory_space=pltpu.SMEM), full(A_WIDTH), full(2 * A_KV_WIDTH),
                  full(2 * B_WIDTH), full(C_WIDTH), full(2 * C_WIDTH),
                  pl.BlockSpec(conv_w.shape, lambda b: (0, 0))],
        out_specs=full(width),
        out_shape=jax.ShapeDtypeStruct((bsz, lctx, width), BF16),
        compiler_params=_cparams(VMEM_LIMIT),
        name="mix_ctx",
    )(sink, qa, kva, ub, qn, kvn, conv_w)


def _na_bias_tables(rpb, nb):
    per_blk = A_BLOCK // GRID_W
    rows = nb * per_blk
    kh = min(NA_ROWS, rows)
    qc = np.arange(GRID_W)[:, None]
    kc = np.arange(GRID_W)[None, :]
    coff = np.clip(kc - qc, -(NA_COLS - 1), NA_COLS - 1) + (NA_COLS - 1)
    cs = np.clip(qc - NA_COLS // 2, 0, GRID_W - NA_COLS)
    col_ok = (kc >= cs) & (kc < cs + NA_COLS)
    onehot = (coff[None] == np.arange(2 * NA_COLS - 1)[:, None, None]).astype(np.float32)
    tiles = jnp.einsum('hrc,cqk->hrqk', rpb, onehot, precision=lax.Precision.HIGHEST)
    tiles = jnp.where(col_ok, tiles, NEG_INF)
    masked = jnp.full((rpb.shape[0], GRID_W, GRID_W), NEG_INF, F32)
    patterns = []
    for blk in (0, 1, 2, nb - 2, nb - 1):
        wb = min(max(blk - 2, 0), nb - WIN_BLOCKS)
        q_rows = []
        for qi in range(per_blk):
            qr = per_blk * blk + qi
            rs = min(max(qr - kh // 2, 0), rows - kh)
            k_tiles = []
            for kj in range(WIN_BLOCKS * per_blk):
                kr = per_blk * wb + kj
                k_tiles.append(tiles[:, kr - qr + NA_ROWS - 1] if rs <= kr < rs + kh else masked)
            q_rows.append(jnp.concatenate(k_tiles, axis=-1))
        patterns.append(jnp.concatenate(q_rows, axis=-2))
    return jnp.stack(patterns)


def _router_affinity(h2b, wr_ref):
    logits = _dot(h2b, wr_ref[...])
    lane = lax.broadcasted_iota(jnp.int32, logits.shape, 1)
    logits = jnp.where(lane < N_EXPERTS, logits, NEG_INF)
    e = jnp.exp(logits - jnp.max(logits, axis=1, keepdims=True))
    return e / jnp.sum(e, axis=1, keepdims=True)


def _outproj_core(o_ref, x_ref, g1_ref, lg_ref, lb_ref, sh_ref, sc_ref, wo_ref, wr_ref, alpha):
    mix = _dot(o_ref[...], wo_ref[...])
    z = alpha * x_ref[...] + g1_ref[...] * mix
    x_mid = _ln(z) * lg_ref[...] + lb_ref[...]
    h2 = _ln(x_mid) * (1.0 + sc_ref[...]) + sh_ref[...]
    return x_mid, h2, _router_affinity(h2.astype(BF16), wr_ref)


def _outproj_body(o_ref, x_ref, g1_ref, lg_ref, lb_ref, sh_ref, sc_ref, wo_ref, wr_ref,
                  xm_ref, h2_ref, afft_ref, *, alpha, packed):
    x_mid, h2, aff = _outproj_core(o_ref, x_ref, g1_ref, lg_ref, lb_ref, sh_ref, sc_ref, wo_ref, wr_ref, alpha)
    xm_ref[...] = x_mid
    if packed:
        half = h2.shape[1] // 2
        h2_ref[...] = pltpu.pack_elementwise([h2[:, :half], h2[:, half:]], packed_dtype=BF16)
    else:
        h2_ref[...] = h2.astype(BF16)
    afft_ref[...] = aff.T[:N_EXPERTS, :]


def _outproj(o, x, g1, lg, lb, sh, sc, wo, wr, tm, alpha, latent):
    bsz, seq, d = x.shape
    tok = lambda width: pl.BlockSpec((None, tm, width), lambda b, i: (b, i, 0))
    vec = pl.BlockSpec((None, 1, d), lambda b, i: (b, 0, 0))
    par = pl.BlockSpec((1, d), lambda b, i: (0, 0))
    afft_spec = pl.BlockSpec((None, N_EXPERTS, tm), lambda b, i: (b, 0, i))
    afft_shape = jax.ShapeDtypeStruct((bsz, N_EXPERTS, seq), F32)
    h2_width, h2_dtype = (d // 2, jnp.uint32) if latent else (d, BF16)
    out_specs = [tok(d), tok(h2_width), afft_spec]
    out_shape = [jax.ShapeDtypeStruct((bsz, seq, d), F32), jax.ShapeDtypeStruct((bsz, seq, h2_width), h2_dtype),
                 afft_shape]
    return pl.pallas_call(
        functools.partial(_outproj_body, alpha=alpha, packed=latent),
        grid=(bsz, seq // tm),
        in_specs=[tok(d), tok(d), vec, par, par, vec, vec,
                  pl.BlockSpec(wo.shape, lambda b, i: (0, 0)), pl.BlockSpec(wr.shape, lambda b, i: (0, 0))],
        out_specs=out_specs,
        out_shape=out_shape,
        compiler_params=_cparams(VMEM_LIMIT),
        name="outproj_latent" if latent else "outproj_ctx",
    )(o, x, g1, lg, lb, sh, sc, wo, wr)


def _kth_largest_bucket(aff, cap, axes):
    shape = list(aff.shape)
    for a in axes:
        shape[a] = 1
    thr = jnp.zeros(shape, jnp.int32)
    for bit in range(30, -1, -1):
        cand = thr | jnp.int32(1 << bit)
        cnt = _count(aff >= lax.bitcast_convert_type(cand, F32), axes)
        thr = jnp.where(cnt >= cap, cand, thr)
    return lax.bitcast_convert_type(thr, F32), lax.bitcast_convert_type(thr + 1, F32)


def _count(mask, axes):
    c = jnp.where(mask, 1.0, 0.0)
    for a in sorted(axes, reverse=True):
        c = jnp.sum(c, axis=a, keepdims=True)
    return c


def _select_latent_body(a_ref, u_ref, lt_ref, lb_ref, idx_ref, gate_ref, *, cap, nc):
    aff = a_ref[...]
    rows = aff.shape[0]
    aff3 = aff.reshape(N_EXPERTS, nc, LANES)
    t_lo, t_hi = _kth_largest_bucket(aff3, cap, (1, 2))
    gt = aff3 >= t_hi
    eq = (aff3 >= t_lo) & (aff3 < t_hi)
    need = cap - _count(gt, (1, 2))
    eq_f = jnp.where(eq, 1.0, 0.0).reshape(rows, LANES)
    eq_b = eq_f.astype(BF16)
    rank = (_dot(eq_b, u_ref[...]) - eq_f
            + jnp.sum(_dot(lb_ref[...], eq_b), axis=1, keepdims=True))
    sel3 = gt | (eq & (rank.reshape(N_EXPERTS, nc, LANES) < need))
    sel_f = jnp.where(sel3, 1.0, 0.0).reshape(rows, LANES)
    sel_b = sel_f.astype(BF16)
    cnt = jnp.sum(sel_f, axis=1, keepdims=True)
    g_incl = jnp.sum(_dot(lb_ref[...], sel_b), axis=1, keepdims=True) + cnt
    r_row = lax.broadcasted_iota(jnp.int32, (1, cap), 1).astype(F32)
    c_col = lax.broadcasted_iota(jnp.int32, (nc, 1), 0).astype(F32)
    l_col = lax.broadcasted_iota(jnp.int32, (LANES, 1), 0).astype(F32)
    for e in range(N_EXPERTS):
        sl = slice(e * nc, (e + 1) * nc)
        before = g_incl[sl] <= r_row
        chunk = jnp.sum(jnp.where(before, 1.0, 0.0), axis=0, keepdims=True)
        base = jnp.sum(jnp.where(before, cnt[sl], 0.0), axis=0, keepdims=True)
        onehot = jnp.where(c_col == chunk, 1.0, 0.0)
        cs_t = _dot_nt(lt_ref[...], sel_b[sl]).astype(BF16)
        cs_of_r = _dot(cs_t, onehot.astype(BF16))
        local = jnp.sum(jnp.where(cs_of_r <= r_row - base, 1.0, 0.0), axis=0, keepdims=True)
        idx_ref[e:e + 1, :] = (chunk * LANES + local).astype(jnp.int32)
        aff_of_r = jnp.dot(aff[sl].T, onehot, precision=lax.Precision.HIGHEST, preferred_element_type=F32)
        gate_ref[e:e + 1, :] = jnp.sum(jnp.where(l_col == local, aff_of_r, 0.0), axis=0, keepdims=True)


def _select_latent(aff_t, cap):
    bsz, n_exp, seq = aff_t.shape
    nc = seq // LANES
    rows = n_exp * nc
    tri = np.triu(np.ones((LANES, LANES), np.float32))
    r = np.arange(rows)
    blockdiag = ((r[:, None] // nc == r[None, :] // nc) & (r[None, :] < r[:, None])).astype(np.float32)
    const = lambda shape: pl.BlockSpec(shape, lambda b: (0, 0))
    return pl.pallas_call(
        functools.partial(_select_latent_body, cap=cap, nc=nc),
        grid=(bsz,),
        in_specs=[pl.BlockSpec((None, rows, LANES), lambda b: (b, 0, 0)),
                  const((LANES, LANES)), const((LANES, LANES)), const((rows, rows))],
        out_specs=[pl.BlockSpec((None, n_exp, cap), lambda b: (b, 0, 0))] * 2,
        out_shape=[jax.ShapeDtypeStruct((bsz, n_exp, cap), jnp.int32),
                   jax.ShapeDtypeStruct((bsz, n_exp, cap), F32)],
        compiler_params=_cparams(VMEM_LIMIT),
        name="select_latent",
    )(aff_t.reshape(bsz, rows, LANES), jnp.asarray(tri, BF16), jnp.asarray(tri.T, BF16),
      jnp.asarray(blockdiag, BF16))


def _select_ctx_body(a_ref, u_ref, w_ref, *, cap):
    aff = a_ref[...]
    t_lo, t_hi = _kth_largest_bucket(aff, cap, (1,))
    gt = aff >= t_hi
    eq = (aff >= t_lo) & (aff < t_hi)
    need = cap - _count(gt, (1,))
    eq_f = jnp.where(eq, 1.0, 0.0)
    rank = _dot(eq_f.astype(BF16), u_ref[...]) - eq_f
    w_ref[...] = jnp.where(gt | (eq & (rank < need)), aff, 0.0)


def _select_ctx(aff_t, cap):
    bsz, n_exp, lctx = aff_t.shape
    tri = np.triu(np.ones((lctx, lctx), np.float32))
    blk = pl.BlockSpec((None, n_exp, lctx), lambda b: (b, 0, 0))
    return pl.pallas_call(
        functools.partial(_select_ctx_body, cap=cap),
        grid=(bsz,),
        in_specs=[blk, pl.BlockSpec((lctx, lctx), lambda b: (0, 0))],
        out_specs=blk,
        out_shape=jax.ShapeDtypeStruct((bsz, n_exp, lctx), F32),
        name="select_ctx",
    )(aff_t, jnp.asarray(tri, BF16))


def _swiglu(xe, wg_ref, wu_ref, wd_ref):
    a = _dot(xe, wg_ref[...].astype(BF16))
    u = _dot(xe, wu_ref[...].astype(BF16))
    act = (a * jax.nn.sigmoid(a) * u).astype(BF16)
    return _dot(act, wd_ref[...].astype(BF16))


def _expert_gather_body(idx_ref, nxt_ref, hp_ref, wg_ref, wu_ref, wd_ref, y_ref, xa_ref, xb_ref, *, cap):
    e = pl.program_id(1)
    groups = cap // SUBLANES

    def gather_group(rows_ref, dst_ref, g):
        for k in range(SUBLANES):
            dst_ref[g, k:k + 1, :] = hp_ref[pl.ds(rows_ref[0, 0, g * SUBLANES + k], 1), :]

    @pl.when(e == 0)
    def _():
        def body(g, carry):
            gather_group(idx_ref, xa_ref, g)
            return carry
        lax.fori_loop(0, groups, body, 0)

    def ffn_and_prefetch(cur_ref, nxt_buf_ref):
        packed = cur_ref[...].reshape(cap, cur_ref.shape[-1])
        halves = [pltpu.unpack_elementwise(packed, index=i, packed_dtype=BF16, unpacked_dtype=F32).astype(BF16)
                  for i in range(2)]
        xe = jnp.concatenate(halves, axis=1)
        ff, d = wd_ref.shape
        n_pieces = 2 * (ff // MXU_COLS) + d // MXU_COLS
        per_piece = groups // n_pieces
        done = [0]

        def prefetch_slice(last=False):
            stop = groups if last else done[0] + per_piece
            for g in range(done[0], stop):
                gather_group(nxt_ref, nxt_buf_ref, g)
            done[0] = stop

        acts = []
        for j in range(ff // MXU_COLS):
            cols = slice(j * MXU_COLS, (j + 1) * MXU_COLS)
            a = _dot(xe, wg_ref[:, cols].astype(BF16))
            prefetch_slice()
            u = _dot(xe, wu_ref[:, cols].astype(BF16))
            prefetch_slice()
            acts.append((a * jax.nn.sigmoid(a) * u).astype(BF16))
        act = jnp.concatenate(acts, axis=1)
        for j in range(d // MXU_COLS):
            cols = slice(j * MXU_COLS, (j + 1) * MXU_COLS)
            y_ref[:, cols] = _dot(act, wd_ref[:, cols].astype(BF16))
            prefetch_slice(last=j == d // MXU_COLS - 1)

    @pl.when(e % 2 == 0)
    def _():
        ffn_and_prefetch(xa_ref, xb_ref)

    @pl.when(e % 2 == 1)
    def _():
        ffn_and_prefetch(xb_ref, xa_ref)


def _expert_gather(idx, hp, wg, wu, wd, layer):
    bsz, n_exp, cap = idx.shape
    seq, width = hp.shape[1:]
    d, ff = wg.shape[2:]
    assert n_exp % 2 == 0
    rows = pltpu.VMEM((cap // SUBLANES, SUBLANES, width), jnp.uint32)
    return pl.pallas_call(
        functools.partial(_expert_gather_body, cap=cap),
        grid=(bsz, n_exp),
        in_specs=[
            pl.BlockSpec((1, 1, cap), lambda b, e: (b * n_exp + e, 0, 0), memory_space=pltpu.SMEM),
            pl.BlockSpec((1, 1, cap), lambda b, e: (b * n_exp + jnp.minimum(e + 1, n_exp - 1), 0, 0),
                         memory_space=pltpu.SMEM),
            pl.BlockSpec((None, seq, width), lambda b, e: (b, 0, 0), pipeline_mode=pl.Buffered(1)),
            pl.BlockSpec((None, None, d, ff), lambda b, e: (layer, e, 0, 0)),
            pl.BlockSpec((None, None, d, ff), lambda b, e: (layer, e, 0, 0)),
            pl.BlockSpec((None, None, ff, d), lambda b, e: (layer, e, 0, 0)),
        ],
        out_specs=pl.BlockSpec((None, None, cap, d), lambda b, e: (b, e, 0, 0)),
        out_shape=jax.ShapeDtypeStruct((bsz, n_exp, cap, d), F32),
        scratch_shapes=[rows, rows],
        compiler_params=_cparams(VMEM_LIMIT),
        name="expert_gather",
    )(idx.reshape(bsz * n_exp, 1, cap), idx.reshape(bsz * n_exp, 1, cap), hp, wg, wu, wd)


def _scatter_body(idx_ref, gate_ref, y_ref, o_ref, *, cap):
    @pl.when(pl.program_id(1) == 0)
    def _():
        o_ref[...] = jnp.zeros_like(o_ref)

    def add_group(g, carry):
        r0 = g * SUBLANES
        ts = [idx_ref[0, 0, r0 + k] for k in range(SUBLANES)]
        acc = [o_ref[pl.ds(ts[k], 1), :] + gate_ref[0, 0, r0 + k] * y_ref[g, k:k + 1, :]
               for k in range(SUBLANES)]
        for k in range(SUBLANES):
            o_ref[pl.ds(ts[k], 1), :] = acc[k]
        return carry

    lax.fori_loop(0, cap // SUBLANES, add_group, 0)


def _scatter_add(idx, gate, y, seq):
    bsz, n_exp, cap, d = y.shape
    sel = lambda b, e: (b * n_exp + e, 0, 0)
    return pl.pallas_call(
        functools.partial(_scatter_body, cap=cap),
        grid=(bsz, n_exp),
        in_specs=[
            pl.BlockSpec((1, 1, cap), sel, memory_space=pltpu.SMEM),
            pl.BlockSpec((1, 1, cap), sel, memory_space=pltpu.SMEM),
            pl.BlockSpec((None, None, cap // SUBLANES, SUBLANES, d), lambda b, e: (b, e, 0, 0, 0)),
        ],
        out_specs=pl.BlockSpec((None, seq, d), lambda b, e: (b, 0, 0), pipeline_mode=pl.Buffered(1)),
        out_shape=jax.ShapeDtypeStruct((bsz, seq, d), F32),
        compiler_params=_cparams(VMEM_LIMIT),
        name="scatter_add",
    )(idx.reshape(bsz * n_exp, 1, cap), gate.reshape(bsz * n_exp, 1, cap),
      y.reshape(bsz, n_exp, cap // SUBLANES, SUBLANES, d))


def _expert_dense_body(h_ref, w_ref, wg_ref, wu_ref, wd_ref, o_ref):
    e = pl.program_id(0)

    @pl.when(e == 0)
    def _():
        o_ref[...] = jnp.zeros_like(o_ref)

    w = w_ref[...]
    lane = lax.broadcasted_iota(jnp.int32, w.shape, 1)
    gate = jnp.sum(jnp.where(lane == e, w, 0.0), axis=1, keepdims=True)
    o_ref[...] += _swiglu(h_ref[...], wg_ref, wu_ref, wd_ref) * gate


def _expert_dense(h, w, wg, wu, wd, layer):
    t, d = h.shape
    n_exp, _, ff = wg.shape[1:]
    return pl.pallas_call(
        _expert_dense_body,
        grid=(n_exp,),
        in_specs=[
            pl.BlockSpec((t, d), lambda e: (0, 0)),
            pl.BlockSpec(w.shape, lambda e: (0, 0)),
            pl.BlockSpec((None, None, d, ff), lambda e: (layer, e, 0, 0)),
            pl.BlockSpec((None, None, d, ff), lambda e: (layer, e, 0, 0)),
            pl.BlockSpec((None, None, ff, d), lambda e: (layer, e, 0, 0)),
        ],
        out_specs=pl.BlockSpec((t, d), lambda e: (0, 0)),
        out_shape=jax.ShapeDtypeStruct((t, d), F32),
        compiler_params=_cparams(VMEM_LIMIT),
        name="expert_dense",
    )(h, w, wg, wu, wd)


def _final_ln_body(xm_ref, f_ref, g2_ref, lg_ref, lb_ref, o_ref, *, alpha):
    z = alpha * xm_ref[...] + g2_ref[...] * f_ref[...]
    o_ref[...] = _ln(z) * lg_ref[...] + lb_ref[...]


def _final_ln(xm, ffn, g2, lg, lb, tm, alpha):
    bsz, seq, d = xm.shape
    tok = pl.BlockSpec((None, tm, d), lambda b, i: (b, i, 0))
    par = pl.BlockSpec((1, d), lambda b, i: (0, 0))
    return pl.pallas_call(
        functools.partial(_final_ln_body, alpha=alpha),
        grid=(bsz, seq // tm),
        in_specs=[tok, tok, pl.BlockSpec((None, 1, d), lambda b, i: (b, 0, 0)), par, par],
        out_specs=tok,
        out_shape=jax.ShapeDtypeStruct((bsz, seq, d), F32),
        compiler_params=_cparams(VMEM_LIMIT),
        name="final_ln",
    )(xm, ffn, g2, lg, lb)


def _rope_tables(seq):
    t = np.arange(seq)
    inv = ROPE_THETA ** (-np.arange(0, ROPE_AXIS_DIM, 2, dtype=np.float32) / ROPE_AXIS_DIM)
    inv = jnp.asarray(inv, F32)

    def axis(pos):
        ang = jnp.asarray(pos, F32)[:, None] * inv[None, :]
        c, s = jnp.cos(ang), jnp.sin(ang)
        return jnp.concatenate([c, c], axis=1), jnp.concatenate([-s, s], axis=1)

    cr, sr = axis(t // GRID_W)
    cc, sc = axis(t % GRID_W)
    reps = LANES // HEAD_DIM
    return (jnp.tile(jnp.concatenate([cr, cc], axis=1), (1, reps)),
            jnp.tile(jnp.concatenate([sr, sc], axis=1), (1, reps)))


def _token_tile(seq):
    return 512 if seq % 512 == 0 else seq


def kernel(x, c, ctx, c_ctx, w_mod, b_mod, w_in, conv_w, attn_sink, na_rpb, w_out, ln1_g, ln1_b,
           w_router, w_gate, w_up, w_down, ln2_g, ln2_b):
    bsz, seq, d = x.shape
    lctx = ctx.shape[1]
    depth = w_mod.shape[0]
    alpha = (2 * depth) ** 0.25
    nb = seq // A_BLOCK
    assert seq % A_BLOCK == 0 and nb >= WIN_BLOCKS + 1 and bsz + 1 <= MOD_ROWS
    tm = _token_tile(seq)
    tmc = _token_tile(bsz * lctx)

    cond = jnp.zeros((MOD_ROWS, d), F32).at[:bsz].set(c).at[bsz].set(c_ctx)
    mod = _modulation(cond, w_mod, b_mod)

    cos, sin = _rope_tables(seq)
    cos_c = jnp.ones((bsz * lctx, LANES), F32)
    sin_c = jnp.zeros((bsz * lctx, LANES), F32)

    head_order = [t * A_GROUP + j for j in range(A_GROUP) for t in range(A_KV_HEADS)]
    a_cols = np.concatenate([np.arange(h * HEAD_DIM, (h + 1) * HEAD_DIM) for h in head_order])
    in_cols = np.concatenate([a_cols, np.arange(A_WIDTH, IN_WIDTH)])
    out_rows = np.concatenate([a_cols, np.arange(A_WIDTH, w_out.shape[1])])

    cap = CAPACITY_FACTOR * seq // N_EXPERTS
    cap_c = CAPACITY_FACTOR * lctx // N_EXPERTS
    ctx_flat = ctx.reshape(1, bsz * lctx, d)

    for l in range(depth):
        last = l == depth - 1
        m6 = mod[l].reshape(MOD_ROWS, N_MOD, d)
        sh1, sc1, g1, sh2, sc2, g2 = (m6[:bsz, i].reshape(bsz, 1, d) for i in range(N_MOD))
        csh1, csc1, cg1, csh2, csc2, cg2 = (m6[bsz:bsz + 1, i].reshape(1, 1, d) for i in range(N_MOD))
        w_in_l = w_in[l][:, in_cols].astype(BF16)
        w_out_l = w_out[l][out_rows, :].astype(BF16)
        w_r = jnp.zeros((d, LANES), BF16).at[:, :N_EXPERTS].set(w_router[l].astype(BF16))
        sink = attn_sink[l]
        lg1, lb1 = ln1_g[l].reshape(1, d), ln1_b[l].reshape(1, d)
        lg2, lb2 = ln2_g[l].reshape(1, d), ln2_b[l].reshape(1, d)

        qa, kva, ub, qn, kvn = _inproj(x, sh1, sc1, cos, sin, w_in_l, tm)
        qa_c, kva_c, ub_c, qn_c, kvn_c = (
            a.reshape(bsz, lctx, a.shape[-1])
            for a in _inproj(ctx_flat, csh1, csc1, cos_c, sin_c, w_in_l, tmc))
        bias = _na_bias_tables(na_rpb[l], nb)
        o = _mix_latent(sink, qa, kva, ub, qn, kvn, kva_c, kvn_c, bias, conv_w[l])
        x_mid, hp, aff_t = _outproj(o, x, g1, lg1, lb1, sh2, sc2, w_out_l, w_r, tm, alpha, True)

        if not last:
            o_c = _mix_ctx(sink, qa_c, kva_c, ub_c, qn_c, kvn_c, conv_w[l])
            ctx_mid, h2c, aff_ct = _outproj(
                o_c.reshape(1, bsz * lctx, d), ctx_flat, cg1, lg1, lb1, csh2, csc2, w_out_l, w_r,
                tmc, alpha, False)
            aff_ct = jnp.transpose(aff_ct.reshape(N_EXPERTS, bsz, lctx), (1, 0, 2))
            gate_c = _select_ctx(aff_ct, cap_c)
            gate_c = jnp.transpose(gate_c, (0, 2, 1)).reshape(bsz * lctx, N_EXPERTS)
            ffn_c = _expert_dense(h2c[0], gate_c, w_gate, w_up, w_down, l)
            ctx_flat = _final_ln(ctx_mid, ffn_c[None], cg2, lg2, lb2, tmc, alpha)

        idx, gate = _select_latent(aff_t, cap)
        y = _expert_gather(idx, hp, w_gate, w_up, w_down, l)
        ffn = _scatter_add(idx, gate, y, seq)
        x = _final_ln(x_mid, ffn, g2, lg2, lb2, tm, alpha)
    return x
```

```python
import functools

import numpy as np
import jax
import jax.numpy as jnp
from jax import lax
from jax.experimental import pallas as pl
from jax.experimental.pallas import tpu as pltpu

HEAD_DIM = 64
GRID_W = 64
A_Q_HEADS = 6
A_KV_HEADS = 2
A_GROUP = A_Q_HEADS // A_KV_HEADS
A_WINDOW = 128
A_BLOCK = 128
B_WIDTH = 256
C_HEADS = 6
NA_ROWS = 8
NA_COLS = 16
A_WIDTH = A_Q_HEADS * HEAD_DIM
A_KV_WIDTH = A_KV_HEADS * HEAD_DIM
C_WIDTH = C_HEADS * HEAD_DIM
N_EXPERTS = 16
CAPACITY_FACTOR = 2
ROPE_THETA = 10000.0
ROPE_AXIS_DIM = HEAD_DIM // 2
LN_EPS = 1e-6
N_MOD = 6
NEG_INF = -1e30

LANES = 128
SUBLANES = 8
MXU_COLS = 256
MOD_ROWS = SUBLANES
HALO_ROWS = 16
WIN_BLOCKS = 5
N_BIAS_PATTERNS = 5
VMEM_LIMIT = 56 * 1024 * 1024

BF16 = jnp.bfloat16
F32 = jnp.float32


def _cparams(vmem=None):
    return pltpu.CompilerParams(vmem_limit_bytes=vmem) if vmem else None


def _dot(a, b):
    return jnp.dot(a, b, preferred_element_type=F32)


def _dot_nt(a, b):
    return lax.dot_general(a, b, (((1,), (1,)), ((), ())), preferred_element_type=F32)


def _ln(x):
    mu = jnp.mean(x, axis=-1, keepdims=True)
    xc = x - mu
    var = jnp.mean(xc * xc, axis=-1, keepdims=True)
    return xc * lax.rsqrt(var + LN_EPS)


def _mod_body(c_ref, w_ref, b_ref, o_ref):
    c = c_ref[...]
    h = (c * jax.nn.sigmoid(c)).astype(BF16)
    o_ref[...] = _dot(h, w_ref[...].astype(BF16)) + b_ref[...]


def _modulation(cond, w_mod, b_mod):
    depth, d, n = w_mod.shape
    tn = n // 4
    return pl.pallas_call(
        _mod_body,
        grid=(depth, n // tn),
        in_specs=[
            pl.BlockSpec((MOD_ROWS, d), lambda l, j: (0, 0)),
            pl.BlockSpec((None, d, tn), lambda l, j: (l, 0, j)),
            pl.BlockSpec((None, 1, tn), lambda l, j: (l, 0, j)),
        ],
        out_specs=pl.BlockSpec((None, MOD_ROWS, tn), lambda l, j: (l, 0, j)),
        out_shape=jax.ShapeDtypeStruct((depth, MOD_ROWS, n), F32),
        compiler_params=_cparams(VMEM_LIMIT),
        name="modulation",
    )(cond, w_mod, b_mod.reshape(depth, 1, n))


QK_W = A_WIDTH + HEAD_DIM * A_KV_HEADS
O_VA = QK_W
O_B = O_VA + A_KV_WIDTH
O_QN = O_B + 3 * B_WIDTH
O_KVN = O_QN + C_WIDTH
IN_WIDTH = O_KVN + 2 * C_WIDTH


def _inproj_body(x_ref, sh_ref, sc_ref, cos_ref, sin_ref, w_ref,
                 qa_ref, kva_ref, ub_ref, qn_ref, kvn_ref):
    h = _ln(x_ref[...]) * (1.0 + sc_ref[...]) + sh_ref[...]
    hb = h.astype(BF16)
    scale = HEAD_DIM ** -0.5
    qk = _dot(hb, w_ref[:, 0:QK_W])
    cos = cos_ref[...]
    sin = sin_ref[...]
    lane = lax.broadcasted_iota(jnp.int32, cos.shape, 1)
    first = (lane % ROPE_AXIS_DIM) < (ROPE_AXIS_DIM // 2)
    half = ROPE_AXIS_DIM // 2
    parts = []
    for j in range(QK_W // LANES):
        t = qk[:, j * LANES:(j + 1) * LANES]
        partner = jnp.where(first, pltpu.roll(t, LANES - half, 1), pltpu.roll(t, half, 1))
        parts.append(t * cos + partner * sin)
    qa_ref[...] = (jnp.concatenate(parts[:-1], axis=1) * scale).astype(BF16)
    va = _dot(hb, w_ref[:, O_VA:O_B])
    kva_ref[...] = jnp.concatenate([parts[-1], va], axis=1).astype(BF16)
    g = _dot(hb, w_ref[:, O_B:O_QN])
    bx, bb, bc = g[:, :B_WIDTH], g[:, B_WIDTH:2 * B_WIDTH], g[:, 2 * B_WIDTH:]
    ub_ref[...] = jnp.concatenate([bc * bx, bb], axis=1).astype(BF16)
    qn_ref[...] = (_dot(hb, w_ref[:, O_QN:O_KVN]) * scale).astype(BF16)
    kvn_ref[...] = _dot(hb, w_ref[:, O_KVN:IN_WIDTH]).astype(BF16)


def _inproj(x, sh, sc, cos, sin, w, tm):
    bsz, seq, d = x.shape
    tok = lambda width: pl.BlockSpec((None, tm, width), lambda b, i: (b, i, 0))
    vec = pl.BlockSpec((None, 1, d), lambda b, i: (b, 0, 0))
    tab = pl.BlockSpec((tm, LANES), lambda b, i: (i, 0))
    widths = (A_WIDTH, 2 * A_KV_WIDTH, 2 * B_WIDTH, C_WIDTH, 2 * C_WIDTH)
    return pl.pallas_call(
        _inproj_body,
        grid=(bsz, seq // tm),
        in_specs=[tok(d), vec, vec, tab, tab, pl.BlockSpec((d, IN_WIDTH), lambda b, i: (0, 0))],
        out_specs=[tok(wd) for wd in widths],
        out_shape=[jax.ShapeDtypeStruct((bsz, seq, wd), BF16) for wd in widths],
        compiler_params=_cparams(VMEM_LIMIT),
        name="inproj",
    )(x, sh, sc, cos, sin, w)


def _split_heads(q, lo):
    zero = jnp.zeros_like(q)
    return jnp.concatenate([jnp.where(lo, q, zero), jnp.where(lo, zero, q)], axis=0)


def _with_ones(v_cat):
    return jnp.concatenate([v_cat, jnp.ones_like(v_cat)], axis=1)


def _attend(units, m_rows):
    lo = lax.broadcasted_iota(jnp.int32, (m_rows, LANES), 1) < HEAD_DIM
    scores = lambda unit: _dot_nt(_split_heads(unit[0], lo), unit[1])
    s_next = scores(units[0])
    outs = []
    for i, (_, _, values, probs) in enumerate(units):
        s = s_next
        if i + 1 < len(units):
            s_next = scores(units[i + 1])
        e, extra = probs(s)
        o = _dot(e, values)
        den = o[:, LANES:] if extra is None else o[:, LANES:] + extra
        o = o[:, :LANES] / den
        outs.append(jnp.where(lo, o[:m_rows], o[m_rows:]))
    return outs


def _gqa_probs(sink_ref, mask, tile, m_rows):
    def probs(s):
        es, sinks = [], []
        for half in range(A_KV_HEADS):
            sb = s[half * m_rows:(half + 1) * m_rows]
            if mask is not None:
                sb = jnp.where(mask, sb, NEG_INF)
            sk = sink_ref[half * A_GROUP + tile]
            m = jnp.maximum(jnp.max(sb, axis=1, keepdims=True), sk)
            es.append(jnp.exp((sb - m).astype(BF16)))
            sinks.append(jnp.exp(sk - m))
        return jnp.concatenate(es, axis=0), jnp.concatenate(sinks, axis=0)
    return probs


def _mha_probs(bias_pair, n_biased):
    def probs(s):
        if bias_pair is None:
            return jnp.exp((s - jnp.max(s, axis=1, keepdims=True)).astype(BF16)), None
        s_w = s[:, :n_biased] + bias_pair
        s_c = s[:, n_biased:]
        m = jnp.maximum(jnp.max(s_w, axis=1, keepdims=True), jnp.max(s_c, axis=1, keepdims=True))
        return jnp.concatenate([jnp.exp((s_w - m).astype(BF16)), jnp.exp((s_c - m).astype(BF16))], axis=1), None
    return probs


def _short_conv(ub, prev_row, next_row, w_ref, m_rows):
    u = ub[:, :B_WIDTH].astype(F32)
    bb = ub[:, B_WIDTH:].astype(F32)
    row = lax.broadcasted_iota(jnp.int32, u.shape, 0)
    u_m1 = jnp.where(row == 0, prev_row, pltpu.roll(u, 1, 0))
    u_p1 = jnp.where(row == m_rows - 1, next_row, pltpu.roll(u, m_rows - 1, 0))
    y = u_m1 * w_ref[0:1, :] + u * w_ref[1:2, :] + u_p1 * w_ref[2:3, :]
    return bb * y


def _mix_latent_body(sink_ref, qa_ref, kp_ref, ko_ref, kn_ref, ubp_ref, ubo_ref, ubn_ref, qn_ref,
                     w0_ref, w1_ref, w2_ref, w3_ref, w4_ref, kvac_ref, kvnc_ref, bias_ref, cw_ref,
                     o_ref, *, seq):
    n = pl.program_id(1)
    nb = pl.num_programs(1)
    m_rows = A_BLOCK
    k_cat = jnp.concatenate([kp_ref[:, :A_KV_WIDTH], ko_ref[:, :A_KV_WIDTH], kn_ref[:, :A_KV_WIDTH],
                             kvac_ref[:, :A_KV_WIDTH]], axis=0)
    v_cat = jnp.concatenate([kp_ref[:, A_KV_WIDTH:], ko_ref[:, A_KV_WIDTH:], kn_ref[:, A_KV_WIDTH:],
                             kvac_ref[:, A_KV_WIDTH:]], axis=0)
    n_keys = k_cat.shape[0]
    qpos = n * A_BLOCK + lax.broadcasted_iota(jnp.int32, (m_rows, n_keys), 0)
    col = lax.broadcasted_iota(jnp.int32, (m_rows, n_keys), 1)
    kpos = (n - 1) * A_BLOCK + col
    mask = (col >= 3 * A_BLOCK) | ((jnp.abs(qpos - kpos) <= A_WINDOW) & (kpos >= 0) & (kpos < seq))
    v_ones = _with_ones(v_cat)
    units = [(qa_ref[:, j * LANES:(j + 1) * LANES], k_cat, v_ones, _gqa_probs(sink_ref, mask, j, m_rows))
             for j in range(A_WIDTH // LANES)]
    wins = (w0_ref, w1_ref, w2_ref, w3_ref, w4_ref)
    n_win = WIN_BLOCKS * A_BLOCK
    for j in range(C_WIDTH // LANES):
        ks = slice(j * LANES, (j + 1) * LANES)
        vs = slice(C_WIDTH + j * LANES, C_WIDTH + (j + 1) * LANES)
        kc = jnp.concatenate([w[:, ks] for w in wins] + [kvnc_ref[:, ks]], axis=0)
        vc = jnp.concatenate([w[:, vs] for w in wins] + [kvnc_ref[:, vs]], axis=0)
        bias_pair = jnp.concatenate([bias_ref[2 * j], bias_ref[2 * j + 1]], axis=0)
        units.append((qn_ref[:, ks], kc, _with_ones(vc), _mha_probs(bias_pair, n_win)))
    outs = _attend(units, m_rows)
    prev_row = jnp.where(n > 0, ubp_ref[HALO_ROWS - 1:HALO_ROWS, :B_WIDTH].astype(F32), 0.0)
    next_row = jnp.where(n < nb - 1, ubn_ref[0:1, :B_WIDTH].astype(F32), 0.0)
    o_b = _short_conv(ubo_ref[...], prev_row, next_row, cw_ref, m_rows)
    n_a = A_WIDTH // LANES
    o_ref[...] = jnp.concatenate(outs[:n_a] + [o_b] + outs[n_a:], axis=1).astype(BF16)


def _mix_latent(sink, qa, kva, ub, qn, kvn, kva_c, kvn_c, bias, conv_w):
    bsz, seq, _ = qa.shape
    nb = seq // A_BLOCK
    lctx = kva_c.shape[1]
    blk = A_BLOCK
    hb = blk // HALO_ROWS
    n_halo = seq // HALO_ROWS
    tok = lambda width, fn: pl.BlockSpec((None, blk, width), fn)
    own = lambda b, n: (b, n, 0)
    win = lambda i: (lambda b, n: (b, jnp.clip(n - 2, 0, nb - WIN_BLOCKS) + i, 0))
    pattern = lambda b, n: (jnp.minimum(n, 2) + jnp.maximum(n - (nb - 3), 0), 0, 0, 0)
    in_specs = [
        pl.BlockSpec(memory_space=pltpu.SMEM),
        tok(A_WIDTH, own),
        tok(2 * A_KV_WIDTH, lambda b, n: (b, jnp.maximum(n - 1, 0), 0)),
        tok(2 * A_KV_WIDTH, own),
        tok(2 * A_KV_WIDTH, lambda b, n: (b, jnp.minimum(n + 1, nb - 1), 0)),
        pl.BlockSpec((None, HALO_ROWS, 2 * B_WIDTH), lambda b, n: (b, jnp.maximum(n * hb - 1, 0), 0)),
        tok(2 * B_WIDTH, own),
        pl.BlockSpec((None, HALO_ROWS, 2 * B_WIDTH), lambda b, n: (b, jnp.minimum((n + 1) * hb, n_halo - 1), 0)),
        tok(C_WIDTH, own),
    ] + [tok(2 * C_WIDTH, win(i)) for i in range(WIN_BLOCKS)] + [
        pl.BlockSpec((None, lctx, 2 * A_KV_WIDTH), lambda b, n: (b, 0, 0)),
        pl.BlockSpec((None, lctx, 2 * C_WIDTH), lambda b, n: (b, 0, 0)),
        pl.BlockSpec((None, C_HEADS, blk, WIN_BLOCKS * blk), pattern),
        pl.BlockSpec(conv_w.shape, lambda b, n: (0, 0)),
    ]
    return pl.pallas_call(
        functools.partial(_mix_latent_body, seq=seq),
        grid=(bsz, nb),
        in_specs=in_specs,
        out_specs=tok(A_WIDTH + B_WIDTH + C_WIDTH, own),
        out_shape=jax.ShapeDtypeStruct((bsz, seq, A_WIDTH + B_WIDTH + C_WIDTH), BF16),
        compiler_params=_cparams(VMEM_LIMIT),
        name="mix_latent",
    )(sink, qa, kva, kva, kva, ub, ub, ub, qn, kvn, kvn, kvn, kvn, kvn, kva_c, kvn_c, bias, conv_w)


def _mix_ctx_body(sink_ref, qa_ref, kva_ref, ub_ref, qn_ref, kvn_ref, cw_ref, o_ref):
    m_rows = qa_ref.shape[0]
    v_ones = _with_ones(kva_ref[:, A_KV_WIDTH:])
    units = [(qa_ref[:, j * LANES:(j + 1) * LANES], kva_ref[:, :A_KV_WIDTH], v_ones,
              _gqa_probs(sink_ref, None, j, m_rows)) for j in range(A_WIDTH // LANES)]
    for j in range(C_WIDTH // LANES):
        ks = slice(j * LANES, (j + 1) * LANES)
        vs = slice(C_WIDTH + j * LANES, C_WIDTH + (j + 1) * LANES)
        units.append((qn_ref[:, ks], kvn_ref[:, ks], _with_ones(kvn_ref[:, vs]), _mha_probs(None, 0)))
    outs = _attend(units, m_rows)
    o_b = _short_conv(ub_ref[...], 0.0, 0.0, cw_ref, m_rows)
    n_a = A_WIDTH // LANES
    o_ref[...] = jnp.concatenate(outs[:n_a] + [o_b] + outs[n_a:], axis=1).astype(BF16)


def _mix_ctx(sink, qa, kva, ub, qn, kvn, conv_w):
    bsz, lctx, _ = qa.shape
    full = lambda width: pl.BlockSpec((None, lctx, width), lambda b: (b, 0, 0))
    width = A_WIDTH + B_WIDTH + C_WIDTH
    return pl.pallas_call(
        _mix_ctx_body,
        grid=(bsz,),
        in_specs=[pl.BlockSpec(memory_space=pltpu.SMEM), full(A_WIDTH), full(2 * A_KV_WIDTH),
                  full(2 * B_WIDTH), full(C_WIDTH), full(2 * C_WIDTH),
                  pl.BlockSpec(conv_w.shape, lambda b: (0, 0))],
        out_specs=full(width),
        out_shape=jax.ShapeDtypeStruct((bsz, lctx, width), BF16),
        compiler_params=_cparams(VMEM_LIMIT),
        name="mix_ctx",
    )(sink, qa, kva, ub, qn, kvn, conv_w)


def _na_bias_tables(rpb, nb):
    per_blk = A_BLOCK // GRID_W
    rows = nb * per_blk
    kh = min(NA_ROWS, rows)
    qc = np.arange(GRID_W)[:, None]
    kc = np.arange(GRID_W)[None, :]
    coff = np.clip(kc - qc, -(NA_COLS - 1), NA_COLS - 1) + (NA_COLS - 1)
    cs = np.clip(qc - NA_COLS // 2, 0, GRID_W - NA_COLS)
    col_ok = (kc >= cs) & (kc < cs + NA_COLS)
    onehot = (coff[None] == np.arange(2 * NA_COLS - 1)[:, None, None]).astype(np.float32)
    tiles = jnp.einsum('hrc,cqk->hrqk', rpb, onehot, precision=lax.Precision.HIGHEST)
    tiles = jnp.where(col_ok, tiles, NEG_INF)
    masked = jnp.full((rpb.shape[0], GRID_W, GRID_W), NEG_INF, F32)
    patterns = []
    for blk in (0, 1, 2, nb - 2, nb - 1):
        wb = min(max(blk - 2, 0), nb - WIN_BLOCKS)
        q_rows = []
        for qi in range(per_blk):
            qr = per_blk * blk + qi
            rs = min(max(qr - kh // 2, 0), rows - kh)
            k_tiles = []
            for kj in range(WIN_BLOCKS * per_blk):
                kr = per_blk * wb + kj
                k_tiles.append(tiles[:, kr - qr + NA_ROWS - 1] if rs <= kr < rs + kh else masked)
            q_rows.append(jnp.concatenate(k_tiles, axis=-1))
        patterns.append(jnp.concatenate(q_rows, axis=-2))
    return jnp.stack(patterns)


def _router_affinity(h2b, wr_ref):
    logits = _dot(h2b, wr_ref[...])
    lane = lax.broadcasted_iota(jnp.int32, logits.shape, 1)
    logits = jnp.where(lane < N_EXPERTS, logits, NEG_INF)
    e = jnp.exp(logits - jnp.max(logits, axis=1, keepdims=True))
    return e / jnp.sum(e, axis=1, keepdims=True)


def _outproj_core(o_ref, x_ref, g1_ref, lg_ref, lb_ref, sh_ref, sc_ref, wo_ref, wr_ref, alpha):
    mix = _dot(o_ref[...], wo_ref[...])
    z = alpha * x_ref[...] + g1_ref[...] * mix
    x_mid = _ln(z) * lg_ref[...] + lb_ref[...]
    h2 = _ln(x_mid) * (1.0 + sc_ref[...]) + sh_ref[...]
    return x_mid, h2, _router_affinity(h2.astype(BF16), wr_ref)


def _outproj_body(o_ref, x_ref, g1_ref, lg_ref, lb_ref, sh_ref, sc_ref, wo_ref, wr_ref,
                  xm_ref, h2_ref, afft_ref, *, alpha, packed):
    x_mid, h2, aff = _outproj_core(o_ref, x_ref, g1_ref, lg_ref, lb_ref, sh_ref, sc_ref, wo_ref, wr_ref, alpha)
    xm_ref[...] = x_mid
    if packed:
        half = h2.shape[1] // 2
        h2_ref[...] = pltpu.pack_elementwise([h2[:, :half], h2[:, half:]], packed_dtype=BF16)
    else:
        h2_ref[...] = h2.astype(BF16)
    afft_ref[...] = aff.T[:N_EXPERTS, :]


def _outproj(o, x, g1, lg, lb, sh, sc, wo, wr, tm, alpha, latent):
    bsz, seq, d = x.shape
    tok = lambda width: pl.BlockSpec((None, tm, width), lambda b, i: (b, i, 0))
    vec = pl.BlockSpec((None, 1, d), lambda b, i: (b, 0, 0))
    par = pl.BlockSpec((1, d), lambda b, i: (0, 0))
    afft_spec = pl.BlockSpec((None, N_EXPERTS, tm), lambda b, i: (b, 0, i))
    afft_shape = jax.ShapeDtypeStruct((bsz, N_EXPERTS, seq), F32)
    h2_width, h2_dtype = (d // 2, jnp.uint32) if latent else (d, BF16)
    out_specs = [tok(d), tok(h2_width), afft_spec]
    out_shape = [jax.ShapeDtypeStruct((bsz, seq, d), F32), jax.ShapeDtypeStruct((bsz, seq, h2_width), h2_dtype),
                 afft_shape]
    return pl.pallas_call(
        functools.partial(_outproj_body, alpha=alpha, packed=latent),
        grid=(bsz, seq // tm),
        in_specs=[tok(d), tok(d), vec, par, par, vec, vec,
                  pl.BlockSpec(wo.shape, lambda b, i: (0, 0)), pl.BlockSpec(wr.shape, lambda b, i: (0, 0))],
        out_specs=out_specs,
        out_shape=out_shape,
        compiler_params=_cparams(VMEM_LIMIT),
        name="outproj_latent" if latent else "outproj_ctx",
    )(o, x, g1, lg, lb, sh, sc, wo, wr)


def _kth_largest_bucket(aff, cap, axes):
    shape = list(aff.shape)
    for a in axes:
        shape[a] = 1
    thr = jnp.zeros(shape, jnp.int32)
    for bit in range(30, -1, -1):
        cand = thr | jnp.int32(1 << bit)
        cnt = _count(aff >= lax.bitcast_convert_type(cand, F32), axes)
        thr = jnp.where(cnt >= cap, cand, thr)
    return lax.bitcast_convert_type(thr, F32), lax.bitcast_convert_type(thr + 1, F32)


def _count(mask, axes):
    c = jnp.where(mask, 1.0, 0.0)
    for a in sorted(axes, reverse=True):
        c = jnp.sum(c, axis=a, keepdims=True)
    return c


def _select_latent_body(a_ref, u_ref, lt_ref, lb_ref, idx_ref, gate_ref, *, cap, nc):
    aff = a_ref[...]
    rows = aff.shape[0]
    aff3 = aff.reshape(N_EXPERTS, nc, LANES)
    t_lo, t_hi = _kth_largest_bucket(aff3, cap, (1, 2))
    gt = aff3 >= t_hi
    eq = (aff3 >= t_lo) & (aff3 < t_hi)
    need = cap - _count(gt, (1, 2))
    eq_f = jnp.where(eq, 1.0, 0.0).reshape(rows, LANES)
    eq_b = eq_f.astype(BF16)
    rank = (_dot(eq_b, u_ref[...]) - eq_f
            + jnp.sum(_dot(lb_ref[...], eq_b), axis=1, keepdims=True))
    sel3 = gt | (eq & (rank.reshape(N_EXPERTS, nc, LANES) < need))
    sel_f = jnp.where(sel3, 1.0, 0.0).reshape(rows, LANES)
    sel_b = sel_f.astype(BF16)
    cnt = jnp.sum(sel_f, axis=1, keepdims=True)
    g_incl = jnp.sum(_dot(lb_ref[...], sel_b), axis=1, keepdims=True) + cnt
    r_row = lax.broadcasted_iota(jnp.int32, (1, cap), 1).astype(F32)
    c_col = lax.broadcasted_iota(jnp.int32, (nc, 1), 0).astype(F32)
    l_col = lax.broadcasted_iota(jnp.int32, (LANES, 1), 0).astype(F32)
    for e in range(N_EXPERTS):
        sl = slice(e * nc, (e + 1) * nc)
        before = g_incl[sl] <= r_row
        chunk = jnp.sum(jnp.where(before, 1.0, 0.0), axis=0, keepdims=True)
        base = jnp.sum(jnp.where(before, cnt[sl], 0.0), axis=0, keepdims=True)
        onehot = jnp.where(c_col == chunk, 1.0, 0.0)
        cs_t = _dot_nt(lt_ref[...], sel_b[sl]).astype(BF16)
        cs_of_r = _dot(cs_t, onehot.astype(BF16))
        local = jnp.sum(jnp.where(cs_of_r <= r_row - base, 1.0, 0.0), axis=0, keepdims=True)
        idx_ref[e:e + 1, :] = (chunk * LANES + local).astype(jnp.int32)
        aff_of_r = jnp.dot(aff[sl].T, onehot, precision=lax.Precision.HIGHEST, preferred_element_type=F32)
        gate_ref[e:e + 1, :] = jnp.sum(jnp.where(l_col == local, aff_of_r, 0.0), axis=0, keepdims=True)


def _select_latent(aff_t, cap):
    bsz, n_exp, seq = aff_t.shape
    nc = seq // LANES
    rows = n_exp * nc
    tri = np.triu(np.ones((LANES, LANES), np.float32))
    r = np.arange(rows)
    blockdiag = ((r[:, None] // nc == r[None, :] // nc) & (r[None, :] < r[:, None])).astype(np.float32)
    const = lambda shape: pl.BlockSpec(shape, lambda b: (0, 0))
    return pl.pallas_call(
        functools.partial(_select_latent_body, cap=cap, nc=nc),
        grid=(bsz,),
        in_specs=[pl.BlockSpec((None, rows, LANES), lambda b: (b, 0, 0)),
                  const((LANES, LANES)), const((LANES, LANES)), const((rows, rows))],
        out_specs=[pl.BlockSpec((None, n_exp, cap), lambda b: (b, 0, 0))] * 2,
        out_shape=[jax.ShapeDtypeStruct((bsz, n_exp, cap), jnp.int32),
                   jax.ShapeDtypeStruct((bsz, n_exp, cap), F32)],
        compiler_params=_cparams(VMEM_LIMIT),
        name="select_latent",
    )(aff_t.reshape(bsz, rows, LANES), jnp.asarray(tri, BF16), jnp.asarray(tri.T, BF16),
      jnp.asarray(blockdiag, BF16))


def _select_ctx_body(a_ref, u_ref, w_ref, *, cap):
    aff = a_ref[...]
    t_lo, t_hi = _kth_largest_bucket(aff, cap, (1,))
    gt = aff >= t_hi
    eq = (aff >= t_lo) & (aff < t_hi)
    need = cap - _count(gt, (1,))
    eq_f = jnp.where(eq, 1.0, 0.0)
    rank = _dot(eq_f.astype(BF16), u_ref[...]) - eq_f
    w_ref[...] = jnp.where(gt | (eq & (rank < need)), aff, 0.0)


def _select_ctx(aff_t, cap):
    bsz, n_exp, lctx = aff_t.shape
    tri = np.triu(np.ones((lctx, lctx), np.float32))
    blk = pl.BlockSpec((None, n_exp, lctx), lambda b: (b, 0, 0))
    return pl.pallas_call(
        functools.partial(_select_ctx_body, cap=cap),
        grid=(bsz,),
        in_specs=[blk, pl.BlockSpec((lctx, lctx), lambda b: (0, 0))],
        out_specs=blk,
        out_shape=jax.ShapeDtypeStruct((bsz, n_exp, lctx), F32),
        name="select_ctx",
    )(aff_t, jnp.asarray(tri, BF16))


def _swiglu(xe, wg_ref, wu_ref, wd_ref):
    a = _dot(xe, wg_ref[...].astype(BF16))
    u = _dot(xe, wu_ref[...].astype(BF16))
    act = (a * jax.nn.sigmoid(a) * u).astype(BF16)
    return _dot(act, wd_ref[...].astype(BF16))


def _expert_gather_body(idx_ref, nxt_ref, hp_ref, wg_ref, wu_ref, wd_ref, y_ref, xa_ref, xb_ref, *, cap):
    e = pl.program_id(1)
    groups = cap // SUBLANES

    def gather_group(rows_ref, dst_ref, g):
        for k in range(SUBLANES):
            dst_ref[g, k:k + 1, :] = hp_ref[pl.ds(rows_ref[0, 0, g * SUBLANES + k], 1), :]

    @pl.when(e == 0)
    def _():
        def body(g, carry):
            gather_group(idx_ref, xa_ref, g)
            return carry
        lax.fori_loop(0, groups, body, 0)

    def ffn_and_prefetch(cur_ref, nxt_buf_ref):
        packed = cur_ref[...].reshape(cap, cur_ref.shape[-1])
        halves = [pltpu.unpack_elementwise(packed, index=i, packed_dtype=BF16, unpacked_dtype=F32).astype(BF16)
                  for i in range(2)]
        xe = jnp.concatenate(halves, axis=1)
        ff, d = wd_ref.shape
        n_pieces = 2 * (ff // MXU_COLS) + d // MXU_COLS
        per_piece = groups // n_pieces
        done = [0]

        def prefetch_slice(last=False):
            stop = groups if last else done[0] + per_piece
            for g in range(done[0], stop):
                gather_group(nxt_ref, nxt_buf_ref, g)
            done[0] = stop

        acts = []
        for j in range(ff // MXU_COLS):
            cols = slice(j * MXU_COLS, (j + 1) * MXU_COLS)
            a = _dot(xe, wg_ref[:, cols].astype(BF16))
            prefetch_slice()
            u = _dot(xe, wu_ref[:, cols].astype(BF16))
            prefetch_slice()
            acts.append((a * jax.nn.sigmoid(a) * u).astype(BF16))
        act = jnp.concatenate(acts, axis=1)
        for j in range(d // MXU_COLS):
            cols = slice(j * MXU_COLS, (j + 1) * MXU_COLS)
            y_ref[:, cols] = _dot(act, wd_ref[:, cols].astype(BF16))
            prefetch_slice(last=j == d // MXU_COLS - 1)

    @pl.when(e % 2 == 0)
    def _():
        ffn_and_prefetch(xa_ref, xb_ref)

    @pl.when(e % 2 == 1)
    def _():
        ffn_and_prefetch(xb_ref, xa_ref)


def _expert_gather(idx, hp, wg, wu, wd, layer):
    bsz, n_exp, cap = idx.shape
    seq, width = hp.shape[1:]
    d, ff = wg.shape[2:]
    assert n_exp % 2 == 0
    rows = pltpu.VMEM((cap // SUBLANES, SUBLANES, width), jnp.uint32)
    return pl.pallas_call(
        functools.partial(_expert_gather_body, cap=cap),
        grid=(bsz, n_exp),
        in_specs=[
            pl.BlockSpec((1, 1, cap), lambda b, e: (b * n_exp + e, 0, 0), memory_space=pltpu.SMEM),
            pl.BlockSpec((1, 1, cap), lambda b, e: (b * n_exp + jnp.minimum(e + 1, n_exp - 1), 0, 0),
                         memory_space=pltpu.SMEM),
            pl.BlockSpec((None, seq, width), lambda b, e: (b, 0, 0), pipeline_mode=pl.Buffered(1)),
            pl.BlockSpec((None, None, d, ff), lambda b, e: (layer, e, 0, 0)),
            pl.BlockSpec((None, None, d, ff), lambda b, e: (layer, e, 0, 0)),
            pl.BlockSpec((None, None, ff, d), lambda b, e: (layer, e, 0, 0)),
        ],
        out_specs=pl.BlockSpec((None, None, cap, d), lambda b, e: (b, e, 0, 0)),
        out_shape=jax.ShapeDtypeStruct((bsz, n_exp, cap, d), F32),
        scratch_shapes=[rows, rows],
        compiler_params=_cparams(VMEM_LIMIT),
        name="expert_gather",
    )(idx.reshape(bsz * n_exp, 1, cap), idx.reshape(bsz * n_exp, 1, cap), hp, wg, wu, wd)


def _scatter_body(idx_ref, gate_ref, y_ref, o_ref, *, cap):
    @pl.when(pl.program_id(1) == 0)
    def _():
        o_ref[...] = jnp.zeros_like(o_ref)

    def add_group(g, carry):
        r0 = g * SUBLANES
        ts = [idx_ref[0, 0, r0 + k] for k in range(SUBLANES)]
        acc = [o_ref[pl.ds(ts[k], 1), :] + gate_ref[0, 0, r0 + k] * y_ref[g, k:k + 1, :]
               for k in range(SUBLANES)]
        for k in range(SUBLANES):
            o_ref[pl.ds(ts[k], 1), :] = acc[k]
        return carry

    lax.fori_loop(0, cap // SUBLANES, add_group, 0)


def _scatter_add(idx, gate, y, seq):
    bsz, n_exp, cap, d = y.shape
    sel = lambda b, e: (b * n_exp + e, 0, 0)
    return pl.pallas_call(
        functools.partial(_scatter_body, cap=cap),
        grid=(bsz, n_exp),
        in_specs=[
            pl.BlockSpec((1, 1, cap), sel, memory_space=pltpu.SMEM),
            pl.BlockSpec((1, 1, cap), sel, memory_space=pltpu.SMEM),
            pl.BlockSpec((None, None, cap // SUBLANES, SUBLANES, d), lambda b, e: (b, e, 0, 0, 0)),
        ],
        out_specs=pl.BlockSpec((None, seq, d), lambda b, e: (b, 0, 0), pipeline_mode=pl.Buffered(1)),
        out_shape=jax.ShapeDtypeStruct((bsz, seq, d), F32),
        compiler_params=_cparams(VMEM_LIMIT),
        name="scatter_add",
    )(idx.reshape(bsz * n_exp, 1, cap), gate.reshape(bsz * n_exp, 1, cap),
      y.reshape(bsz, n_exp, cap // SUBLANES, SUBLANES, d))


def _expert_dense_body(h_ref, w_ref, wg_ref, wu_ref, wd_ref, o_ref):
    e = pl.program_id(0)

    @pl.when(e == 0)
    def _():
        o_ref[...] = jnp.zeros_like(o_ref)

    w = w_ref[...]
    lane = lax.broadcasted_iota(jnp.int32, w.shape, 1)
    gate = jnp.sum(jnp.where(lane == e, w, 0.0), axis=1, keepdims=True)
    o_ref[...] += _swiglu(h_ref[...], wg_ref, wu_ref, wd_ref) * gate


def _expert_dense(h, w, wg, wu, wd, layer):
    t, d = h.shape
    n_exp, _, ff = wg.shape[1:]
    return pl.pallas_call(
        _expert_dense_body,
        grid=(n_exp,),
        in_specs=[
            pl.BlockSpec((t, d), lambda e: (0, 0)),
            pl.BlockSpec(w.shape, lambda e: (0, 0)),
            pl.BlockSpec((None, None, d, ff), lambda e: (layer, e, 0, 0)),
            pl.BlockSpec((None, None, d, ff), lambda e: (layer, e, 0, 0)),
            pl.BlockSpec((None, None, ff, d), lambda e: (layer, e, 0, 0)),
        ],
        out_specs=pl.BlockSpec((t, d), lambda e: (0, 0)),
        out_shape=jax.ShapeDtypeStruct((t, d), F32),
        compiler_params=_cparams(VMEM_LIMIT),
        name="expert_dense",
    )(h, w, wg, wu, wd)


def _final_ln_body(xm_ref, f_ref, g2_ref, lg_ref, lb_ref, o_ref, *, alpha):
    z = alpha * xm_ref[...] + g2_ref[...] * f_ref[...]
    o_ref[...] = _ln(z) * lg_ref[...] + lb_ref[...]


def _final_ln(xm, ffn, g2, lg, lb, tm, alpha):
    bsz, seq, d = xm.shape
    tok = pl.BlockSpec((None, tm, d), lambda b, i: (b, i, 0))
    par = pl.BlockSpec((1, d), lambda b, i: (0, 0))
    return pl.pallas_call(
        functools.partial(_final_ln_body, alpha=alpha),
        grid=(bsz, seq // tm),
        in_specs=[tok, tok, pl.BlockSpec((None, 1, d), lambda b, i: (b, 0, 0)), par, par],
        out_specs=tok,
        out_shape=jax.ShapeDtypeStruct((bsz, seq, d), F32),
        compiler_params=_cparams(VMEM_LIMIT),
        name="final_ln",
    )(xm, ffn, g2, lg, lb)


def _rope_tables(seq):
    t = np.arange(seq)
    inv = ROPE_THETA ** (-np.arange(0, ROPE_AXIS_DIM, 2, dtype=np.float32) / ROPE_AXIS_DIM)
    inv = jnp.asarray(inv, F32)

    def axis(pos):
        ang = jnp.asarray(pos, F32)[:, None] * inv[None, :]
        c, s = jnp.cos(ang), jnp.sin(ang)
        return jnp.concatenate([c, c], axis=1), jnp.concatenate([-s, s], axis=1)

    cr, sr = axis(t // GRID_W)
    cc, sc = axis(t % GRID_W)
    reps = LANES // HEAD_DIM
    return (jnp.tile(jnp.concatenate([cr, cc], axis=1), (1, reps)),
            jnp.tile(jnp.concatenate([sr, sc], axis=1), (1, reps)))


def _token_tile(seq):
    return 512 if seq % 512 == 0 else seq


def kernel(x, c, ctx, c_ctx, w_mod, b_mod, w_in, conv_w, attn_sink, na_rpb, w_out, ln1_g, ln1_b,
           w_router, w_gate, w_up, w_down, ln2_g, ln2_b):
    bsz, seq, d = x.shape
    lctx = ctx.shape[1]
    depth = w_mod.shape[0]
    alpha = (2 * depth) ** 0.25
    nb = seq // A_BLOCK
    assert seq % A_BLOCK == 0 and nb >= WIN_BLOCKS + 1 and bsz + 1 <= MOD_ROWS
    tm = _token_tile(seq)
    tmc = _token_tile(bsz * lctx)

    cond = jnp.zeros((MOD_ROWS, d), F32).at[:bsz].set(c).at[bsz].set(c_ctx)
    mod = _modulation(cond, w_mod, b_mod)

    cos, sin = _rope_tables(seq)
    cos_c = jnp.ones((bsz * lctx, LANES), F32)
    sin_c = jnp.zeros((bsz * lctx, LANES), F32)

    head_order = [t * A_GROUP + j for j in range(A_GROUP) for t in range(A_KV_HEADS)]
    a_cols = np.concatenate([np.arange(h * HEAD_DIM, (h + 1) * HEAD_DIM) for h in head_order])
    in_cols = np.concatenate([a_cols, np.arange(A_WIDTH, IN_WIDTH)])
    out_rows = np.concatenate([a_cols, np.arange(A_WIDTH, w_out.shape[1])])

    cap = CAPACITY_FACTOR * seq // N_EXPERTS
    cap_c = CAPACITY_FACTOR * lctx // N_EXPERTS
    ctx_flat = ctx.reshape(1, bsz * lctx, d)

    for l in range(depth):
        last = l == depth - 1
        m6 = mod[l].reshape(MOD_ROWS, N_MOD, d)
        sh1, sc1, g1, sh2, sc2, g2 = (m6[:bsz, i].reshape(bsz, 1, d) for i in range(N_MOD))
        csh1, csc1, cg1, csh2, csc2, cg2 = (m6[bsz:bsz + 1, i].reshape(1, 1, d) for i in range(N_MOD))
        w_in_l = w_in[l][:, in_cols].astype(BF16)
        w_out_l = w_out[l][out_rows, :].astype(BF16)
        w_r = jnp.zeros((d, LANES), BF16).at[:, :N_EXPERTS].set(w_router[l].astype(BF16))
        sink = attn_sink[l]
        lg1, lb1 = ln1_g[l].reshape(1, d), ln1_b[l].reshape(1, d)
        lg2, lb2 = ln2_g[l].reshape(1, d), ln2_b[l].reshape(1, d)

        qa, kva, ub, qn, kvn = _inproj(x, sh1, sc1, cos, sin, w_in_l, tm)
        qa_c, kva_c, ub_c, qn_c, kvn_c = (
            a.reshape(bsz, lctx, a.shape[-1])
            for a in _inproj(ctx_flat, csh1, csc1, cos_c, sin_c, w_in_l, tmc))
        bias = _na_bias_tables(na_rpb[l], nb)
        o = _mix_latent(sink, qa, kva, ub, qn, kvn, kva_c, kvn_c, bias, conv_w[l])
        x_mid, hp, aff_t = _outproj(o, x, g1, lg1, lb1, sh2, sc2, w_out_l, w_r, tm, alpha, True)

        if not last:
            o_c = _mix_ctx(sink, qa_c, kva_c, ub_c, qn_c, kvn_c, conv_w[l])
            ctx_mid, h2c, aff_ct = _outproj(
                o_c.reshape(1, bsz * lctx, d), ctx_flat, cg1, lg1, lb1, csh2, csc2, w_out_l, w_r,
                tmc, alpha, False)
            aff_ct = jnp.transpose(aff_ct.reshape(N_EXPERTS, bsz, lctx), (1, 0, 2))
            gate_c = _select_ctx(aff_ct, cap_c)
            gate_c = jnp.transpose(gate_c, (0, 2, 1)).reshape(bsz * lctx, N_EXPERTS)
            ffn_c = _expert_dense(h2c[0], gate_c, w_gate, w_up, w_down, l)
            ctx_flat = _final_ln(ctx_mid, ffn_c[None], cg2, lg2, lb2, tmc, alpha)

        idx, gate = _select_latent(aff_t, cap)
        y = _expert_gather(idx, hp, w_gate, w_up, w_down, l)
        ffn = _scatter_add(idx, gate, y, seq)
        x = _final_ln(x_mid, ffn, g2, lg2, lb2, tm, alpha)
    return x
```

```python
import functools

import numpy as np
import jax
import jax.numpy as jnp
from jax import lax
from jax.experimental import pallas as pl
from jax.experimental.pallas import tpu as pltpu

HEAD_DIM = 64
GRID_W = 64
A_Q_HEADS = 6
A_KV_HEADS = 2
A_GROUP = A_Q_HEADS // A_KV_HEADS
A_WINDOW = 128
A_BLOCK = 128
B_WIDTH = 256
C_HEADS = 6
NA_ROWS = 8
NA_COLS = 16
A_WIDTH = A_Q_HEADS * HEAD_DIM
A_KV_WIDTH = A_KV_HEADS * HEAD_DIM
C_WIDTH = C_HEADS * HEAD_DIM
N_EXPERTS = 16
CAPACITY_FACTOR = 2
ROPE_THETA = 10000.0
ROPE_AXIS_DIM = HEAD_DIM // 2
LN_EPS = 1e-6
N_MOD = 6
NEG_INF = -1e30

LANES = 128
SUBLANES = 8
MXU_COLS = 256
MOD_ROWS = SUBLANES
HALO_ROWS = 16
WIN_BLOCKS = 5
N_BIAS_PATTERNS = 5
VMEM_LIMIT = 56 * 1024 * 1024

BF16 = jnp.bfloat16
F32 = jnp.float32


def _cparams(vmem=None):
    return pltpu.CompilerParams(vmem_limit_bytes=vmem) if vmem else None


def _dot(a, b):
    return jnp.dot(a, b, preferred_element_type=F32)


def _dot_nt(a, b):
    return lax.dot_general(a, b, (((1,), (1,)), ((), ())), preferred_element_type=F32)


def _ln(x):
    mu = jnp.mean(x, axis=-1, keepdims=True)
    xc = x - mu
    var = jnp.mean(xc * xc, axis=-1, keepdims=True)
    return xc * lax.rsqrt(var + LN_EPS)


def _mod_body(c_ref, w_ref, b_ref, o_ref):
    c = c_ref[...]
    h = (c * jax.nn.sigmoid(c)).astype(BF16)
    o_ref[...] = _dot(h, w_ref[...].astype(BF16)) + b_ref[...]


def _modulation(cond, w_mod, b_mod):
    depth, d, n = w_mod.shape
    tn = n // 4
    return pl.pallas_call(
        _mod_body,
        grid=(depth, n // tn),
        in_specs=[
            pl.BlockSpec((MOD_ROWS, d), lambda l, j: (0, 0)),
            pl.BlockSpec((None, d, tn), lambda l, j: (l, 0, j)),
            pl.BlockSpec((None, 1, tn), lambda l, j: (l, 0, j)),
        ],
        out_specs=pl.BlockSpec((None, MOD_ROWS, tn), lambda l, j: (l, 0, j)),
        out_shape=jax.ShapeDtypeStruct((depth, MOD_ROWS, n), F32),
        compiler_params=_cparams(VMEM_LIMIT),
        name="modulation",
    )(cond, w_mod, b_mod.reshape(depth, 1, n))


QK_W = A_WIDTH + HEAD_DIM * A_KV_HEADS
O_VA = QK_W
O_B = O_VA + A_KV_WIDTH
O_QN = O_B + 3 * B_WIDTH
O_KVN = O_QN + C_WIDTH
IN_WIDTH = O_KVN + 2 * C_WIDTH


def _inproj_body(x_ref, sh_ref, sc_ref, cos_ref, sin_ref, w_ref,
                 qa_ref, kva_ref, ub_ref, qn_ref, kvn_ref):
    h = _ln(x_ref[...]) * (1.0 + sc_ref[...]) + sh_ref[...]
    hb = h.astype(BF16)
    scale = HEAD_DIM ** -0.5
    qk = _dot(hb, w_ref[:, 0:QK_W])
    cos = cos_ref[...]
    sin = sin_ref[...]
    lane = lax.broadcasted_iota(jnp.int32, cos.shape, 1)
    first = (lane % ROPE_AXIS_DIM) < (ROPE_AXIS_DIM // 2)
    half = ROPE_AXIS_DIM // 2
    parts = []
    for j in range(QK_W // LANES):
        t = qk[:, j * LANES:(j + 1) * LANES]
        partner = jnp.where(first, pltpu.roll(t, LANES - half, 1), pltpu.roll(t, half, 1))
        parts.append(t * cos + partner * sin)
    qa_ref[...] = (jnp.concatenate(parts[:-1], axis=1) * scale).astype(BF16)
    va = _dot(hb, w_ref[:, O_VA:O_B])
    kva_ref[...] = jnp.concatenate([parts[-1], va], axis=1).astype(BF16)
    g = _dot(hb, w_ref[:, O_B:O_QN])
    bx, bb, bc = g[:, :B_WIDTH], g[:, B_WIDTH:2 * B_WIDTH], g[:, 2 * B_WIDTH:]
    ub_ref[...] = jnp.concatenate([bc * bx, bb], axis=1).astype(BF16)
    qn_ref[...] = (_dot(hb, w_ref[:, O_QN:O_KVN]) * scale).astype(BF16)
    kvn_ref[...] = _dot(hb, w_ref[:, O_KVN:IN_WIDTH]).astype(BF16)


def _inproj(x, sh, sc, cos, sin, w, tm):
    bsz, seq, d = x.shape
    tok = lambda width: pl.BlockSpec((None, tm, width), lambda b, i: (b, i, 0))
    vec = pl.BlockSpec((None, 1, d), lambda b, i: (b, 0, 0))
    tab = pl.BlockSpec((tm, LANES), lambda b, i: (i, 0))
    widths = (A_WIDTH, 2 * A_KV_WIDTH, 2 * B_WIDTH, C_WIDTH, 2 * C_WIDTH)
    return pl.pallas_call(
        _inproj_body,
        grid=(bsz, seq // tm),
        in_specs=[tok(d), vec, vec, tab, tab, pl.BlockSpec((d, IN_WIDTH), lambda b, i: (0, 0))],
        out_specs=[tok(wd) for wd in widths],
        out_shape=[jax.ShapeDtypeStruct((bsz, seq, wd), BF16) for wd in widths],
        compiler_params=_cparams(VMEM_LIMIT),
        name="inproj",
    )(x, sh, sc, cos, sin, w)


def _split_heads(q, lo):
    zero = jnp.zeros_like(q)
    return jnp.concatenate([jnp.where(lo, q, zero), jnp.where(lo, zero, q)], axis=0)


def _with_ones(v_cat):
    return jnp.concatenate([v_cat, jnp.ones_like(v_cat)], axis=1)


def _attend(units, m_rows):
    lo = lax.broadcasted_iota(jnp.int32, (m_rows, LANES), 1) < HEAD_DIM
    scores = lambda unit: _dot_nt(_split_heads(unit[0], lo), unit[1])
    s_next = scores(units[0])
    outs = []
    for i, (_, _, values, probs) in enumerate(units):
        s = s_next
        if i + 1 < len(units):
            s_next = scores(units[i + 1])
        e, extra = probs(s)
        o = _dot(e, values)
        den = o[:, LANES:] if extra is None else o[:, LANES:] + extra
        o = o[:, :LANES] / den
        outs.append(jnp.where(lo, o[:m_rows], o[m_rows:]))
    return outs


def _gqa_probs(sink_ref, mask, tile, m_rows):
    def probs(s):
        es, sinks = [], []
        for half in range(A_KV_HEADS):
            sb = s[half * m_rows:(half + 1) * m_rows]
            if mask is not None:
                sb = jnp.where(mask, sb, NEG_INF)
            sk = sink_ref[half * A_GROUP + tile]
            m = jnp.maximum(jnp.max(sb, axis=1, keepdims=True), sk)
            es.append(jnp.exp((sb - m).astype(BF16)))
            sinks.append(jnp.exp(sk - m))
        return jnp.concatenate(es, axis=0), jnp.concatenate(sinks, axis=0)
    return probs


def _mha_probs(bias_pair, n_biased):
    def probs(s):
        if bias_pair is None:
            return jnp.exp((s - jnp.max(s, axis=1, keepdims=True)).astype(BF16)), None
        s_w = s[:, :n_biased] + bias_pair
        s_c = s[:, n_biased:]
        m = jnp.maximum(jnp.max(s_w, axis=1, keepdims=True), jnp.max(s_c, axis=1, keepdims=True))
        return jnp.concatenate([jnp.exp((s_w - m).astype(BF16)), jnp.exp((s_c - m).astype(BF16))], axis=1), None
    return probs


def _short_conv(ub, prev_row, next_row, w_ref, m_rows):
    u = ub[:, :B_WIDTH].astype(F32)
    bb = ub[:, B_WIDTH:].astype(F32)
    row = lax.broadcasted_iota(jnp.int32, u.shape, 0)
    u_m1 = jnp.where(row == 0, prev_row, pltpu.roll(u, 1, 0))
    u_p1 = jnp.where(row == m_rows - 1, next_row, pltpu.roll(u, m_rows - 1, 0))
    y = u_m1 * w_ref[0:1, :] + u * w_ref[1:2, :] + u_p1 * w_ref[2:3, :]
    return bb * y


def _mix_latent_body(sink_ref, qa_ref, kp_ref, ko_ref, kn_ref, ubp_ref, ubo_ref, ubn_ref, qn_ref,
                     w0_ref, w1_ref, w2_ref, w3_ref, w4_ref, kvac_ref, kvnc_ref, bias_ref, cw_ref,
                     o_ref, *, seq):
    n = pl.program_id(1)
    nb = pl.num_programs(1)
    m_rows = A_BLOCK
    k_cat = jnp.concatenate([kp_ref[:, :A_KV_WIDTH], ko_ref[:, :A_KV_WIDTH], kn_ref[:, :A_KV_WIDTH],
                             kvac_ref[:, :A_KV_WIDTH]], axis=0)
    v_cat = jnp.concatenate([kp_ref[:, A_KV_WIDTH:], ko_ref[:, A_KV_WIDTH:], kn_ref[:, A_KV_WIDTH:],
                             kvac_ref[:, A_KV_WIDTH:]], axis=0)
    n_keys = k_cat.shape[0]
    qpos = n * A_BLOCK + lax.broadcasted_iota(jnp.int32, (m_rows, n_keys), 0)
    col = lax.broadcasted_iota(jnp.int32, (m_rows, n_keys), 1)
    kpos = (n - 1) * A_BLOCK + col
    mask = (col >= 3 * A_BLOCK) | ((jnp.abs(qpos - kpos) <= A_WINDOW) & (kpos >= 0) & (kpos < seq))
    v_ones = _with_ones(v_cat)
    units = [(qa_ref[:, j * LANES:(j + 1) * LANES], k_cat, v_ones, _gqa_probs(sink_ref, mask, j, m_rows))
             for j in range(A_WIDTH // LANES)]
    wins = (w0_ref, w1_ref, w2_ref, w3_ref, w4_ref)
    n_win = WIN_BLOCKS * A_BLOCK
    for j in range(C_WIDTH // LANES):
        ks = slice(j * LANES, (j + 1) * LANES)
        vs = slice(C_WIDTH + j * LANES, C_WIDTH + (j + 1) * LANES)
        kc = jnp.concatenate([w[:, ks] for w in wins] + [kvnc_ref[:, ks]], axis=0)
        vc = jnp.concatenate([w[:, vs] for w in wins] + [kvnc_ref[:, vs]], axis=0)
        bias_pair = jnp.concatenate([bias_ref[2 * j], bias_ref[2 * j + 1]], axis=0)
        units.append((qn_ref[:, ks], kc, _with_ones(vc), _mha_probs(bias_pair, n_win)))
    outs = _attend(units, m_rows)
    prev_row = jnp.where(n > 0, ubp_ref[HALO_ROWS - 1:HALO_ROWS, :B_WIDTH].astype(F32), 0.0)
    next_row = jnp.where(n < nb - 1, ubn_ref[0:1, :B_WIDTH].astype(F32), 0.0)
    o_b = _short_conv(ubo_ref[...], prev_row, next_row, cw_ref, m_rows)
    n_a = A_WIDTH // LANES
    o_ref[...] = jnp.concatenate(outs[:n_a] + [o_b] + outs[n_a:], axis=1).astype(BF16)


def _mix_latent(sink, qa, kva, ub, qn, kvn, kva_c, kvn_c, bias, conv_w):
    bsz, seq, _ = qa.shape
    nb = seq // A_BLOCK
    lctx = kva_c.shape[1]
    blk = A_BLOCK
    hb = blk // HALO_ROWS
    n_halo = seq // HALO_ROWS
    tok = lambda width, fn: pl.BlockSpec((None, blk, width), fn)
    own = lambda b, n: (b, n, 0)
    win = lambda i: (lambda b, n: (b, jnp.clip(n - 2, 0, nb - WIN_BLOCKS) + i, 0))
    pattern = lambda b, n: (jnp.minimum(n, 2) + jnp.maximum(n - (nb - 3), 0), 0, 0, 0)
    in_specs = [
        pl.BlockSpec(memory_space=pltpu.SMEM),
        tok(A_WIDTH, own),
        tok(2 * A_KV_WIDTH, lambda b, n: (b, jnp.maximum(n - 1, 0), 0)),
        tok(2 * A_KV_WIDTH, own),
        tok(2 * A_KV_WIDTH, lambda b, n: (b, jnp.minimum(n + 1, nb - 1), 0)),
        pl.BlockSpec((None, HALO_ROWS, 2 * B_WIDTH), lambda b, n: (b, jnp.maximum(n * hb - 1, 0), 0)),
        tok(2 * B_WIDTH, own),
        pl.BlockSpec((None, HALO_ROWS, 2 * B_WIDTH), lambda b, n: (b, jnp.minimum((n + 1) * hb, n_halo - 1), 0)),
        tok(C_WIDTH, own),
    ] + [tok(2 * C_WIDTH, win(i)) for i in range(WIN_BLOCKS)] + [
        pl.BlockSpec((None, lctx, 2 * A_KV_WIDTH), lambda b, n: (b, 0, 0)),
        pl.BlockSpec((None, lctx, 2 * C_WIDTH), lambda b, n: (b, 0, 0)),
        pl.BlockSpec((None, C_HEADS, blk, WIN_BLOCKS * blk), pattern),
        pl.BlockSpec(conv_w.shape, lambda b, n: (0, 0)),
    ]
    return pl.pallas_call(
        functools.partial(_mix_latent_body, seq=seq),
        grid=(bsz, nb),
        in_specs=in_specs,
        out_specs=tok(A_WIDTH + B_WIDTH + C_WIDTH, own),
        out_shape=jax.ShapeDtypeStruct((bsz, seq, A_WIDTH + B_WIDTH + C_WIDTH), BF16),
        compiler_params=_cparams(VMEM_LIMIT),
        name="mix_latent",
    )(sink, qa, kva, kva, kva, ub, ub, ub, qn, kvn, kvn, kvn, kvn, kvn, kva_c, kvn_c, bias, conv_w)


def _mix_ctx_body(sink_ref, qa_ref, kva_ref, ub_ref, qn_ref, kvn_ref, cw_ref, o_ref):
    m_rows = qa_ref.shape[0]
    v_ones = _with_ones(kva_ref[:, A_KV_WIDTH:])
    units = [(qa_ref[:, j * LANES:(j + 1) * LANES], kva_ref[:, :A_KV_WIDTH], v_ones,
              _gqa_probs(sink_ref, None, j, m_rows)) for j in range(A_WIDTH // LANES)]
    for j in range(C_WIDTH // LANES):
        ks = slice(j * LANES, (j + 1) * LANES)
        vs = slice(C_WIDTH + j * LANES, C_WIDTH + (j + 1) * LANES)
        units.append((qn_ref[:, ks], kvn_ref[:, ks], _with_ones(kvn_ref[:, vs]), _mha_probs(None, 0)))
    outs = _attend(units, m_rows)
    o_b = _short_conv(ub_ref[...], 0.0, 0.0, cw_ref, m_rows)
    n_a = A_WIDTH // LANES
    o_ref[...] = jnp.concatenate(outs[:n_a] + [o_b] + outs[n_a:], axis=1).astype(BF16)


def _mix_ctx(sink, qa, kva, ub, qn, kvn, conv_w):
    bsz, lctx, _ = qa.shape
    full = lambda width: pl.BlockSpec((None, lctx, width), lambda b: (b, 0, 0))
    width = A_WIDTH + B_WIDTH + C_WIDTH
    return pl.pallas_call(
        _mix_ctx_body,
        grid=(bsz,),
        in_specs=[pl.BlockSpec(memory_space=pltpu.SMEM), full(A_WIDTH), full(2 * A_KV_WIDTH),
                  full(2 * B_WIDTH), full(C_WIDTH), full(2 * C_WIDTH),
                  pl.BlockSpec(conv_w.shape, lambda b: (0, 0))],
        out_specs=full(width),
        out_shape=jax.ShapeDtypeStruct((bsz, lctx, width), BF16),
        compiler_params=_cparams(VMEM_LIMIT),
        name="mix_ctx",
    )(sink, qa, kva, ub, qn, kvn, conv_w)


def _na_bias_tables(rpb, nb):
    per_blk = A_BLOCK // GRID_W
    rows = nb * per_blk
    kh = min(NA_ROWS, rows)
    qc = np.arange(GRID_W)[:, None]
    kc = np.arange(GRID_W)[None, :]
    coff = np.clip(kc - qc, -(NA_COLS - 1), NA_COLS - 1) + (NA_COLS - 1)
    cs = np.clip(qc - NA_COLS // 2, 0, GRID_W - NA_COLS)
    col_ok = (kc >= cs) & (kc < cs + NA_COLS)
    onehot = (coff[None] == np.arange(2 * NA_COLS - 1)[:, None, None]).astype(np.float32)
    tiles = jnp.einsum('hrc,cqk->hrqk', rpb, onehot, precision=lax.Precision.HIGHEST)
    tiles = jnp.where(col_ok, tiles, NEG_INF)
    masked = jnp.full((rpb.shape[0], GRID_W, GRID_W), NEG_INF, F32)
    patterns = []
    for blk in (0, 1, 2, nb - 2, nb - 1):
        wb = min(max(blk - 2, 0), nb - WIN_BLOCKS)
        q_rows = []
        for qi in range(per_blk):
            qr = per_blk * blk + qi
            rs = min(max(qr - kh // 2, 0), rows - kh)
            k_tiles = []
            for kj in range(WIN_BLOCKS * per_blk):
                kr = per_blk * wb + kj
                k_tiles.append(tiles[:, kr - qr + NA_ROWS - 1] if rs <= kr < rs + kh else masked)
            q_rows.append(jnp.concatenate(k_tiles, axis=-1))
        patterns.append(jnp.concatenate(q_rows, axis=-2))
    return jnp.stack(patterns)


def _router_affinity(h2b, wr_ref):
    logits = _dot(h2b, wr_ref[...])
    lane = lax.broadcasted_iota(jnp.int32, logits.shape, 1)
    logits = jnp.where(lane < N_EXPERTS, logits, NEG_INF)
    e = jnp.exp(logits - jnp.max(logits, axis=1, keepdims=True))
    return e / jnp.sum(e, axis=1, keepdims=True)


def _outproj_core(o_ref, x_ref, g1_ref, lg_ref, lb_ref, sh_ref, sc_ref, wo_ref, wr_ref, alpha):
    mix = _dot(o_ref[...], wo_ref[...])
    z = alpha * x_ref[...] + g1_ref[...] * mix
    x_mid = _ln(z) * lg_ref[...] + lb_ref[...]
    h2 = _ln(x_mid) * (1.0 + sc_ref[...]) + sh_ref[...]
    return x_mid, h2, _router_affinity(h2.astype(BF16), wr_ref)


def _outproj_body(o_ref, x_ref, g1_ref, lg_ref, lb_ref, sh_ref, sc_ref, wo_ref, wr_ref,
                  xm_ref, h2_ref, afft_ref, *, alpha, packed):
    x_mid, h2, aff = _outproj_core(o_ref, x_ref, g1_ref, lg_ref, lb_ref, sh_ref, sc_ref, wo_ref, wr_ref, alpha)
    xm_ref[...] = x_mid
    if packed:
        half = h2.shape[1] // 2
        h2_ref[...] = pltpu.pack_elementwise([h2[:, :half], h2[:, half:]], packed_dtype=BF16)
    else:
        h2_ref[...] = h2.astype(BF16)
    afft_ref[...] = aff.T[:N_EXPERTS, :]


def _outproj(o, x, g1, lg, lb, sh, sc, wo, wr, tm, alpha, latent):
    bsz, seq, d = x.shape
    tok = lambda width: pl.BlockSpec((None, tm, width), lambda b, i: (b, i, 0))
    vec = pl.BlockSpec((None, 1, d), lambda b, i: (b, 0, 0))
    par = pl.BlockSpec((1, d), lambda b, i: (0, 0))
    afft_spec = pl.BlockSpec((None, N_EXPERTS, tm), lambda b, i: (b, 0, i))
    afft_shape = jax.ShapeDtypeStruct((bsz, N_EXPERTS, seq), F32)
    h2_width, h2_dtype = (d // 2, jnp.uint32) if latent else (d, BF16)
    out_specs = [tok(d), tok(h2_width), afft_spec]
    out_shape = [jax.ShapeDtypeStruct((bsz, seq, d), F32), jax.ShapeDtypeStruct((bsz, seq, h2_width), h2_dtype),
                 afft_shape]
    return pl.pallas_call(
        functools.partial(_outproj_body, alpha=alpha, packed=latent),
        grid=(bsz, seq // tm),
        in_specs=[tok(d), tok(d), vec, par, par, vec, vec,
                  pl.BlockSpec(wo.shape, lambda b, i: (0, 0)), pl.BlockSpec(wr.shape, lambda b, i: (0, 0))],
        out_specs=out_specs,
        out_shape=out_shape,
        compiler_params=_cparams(VMEM_LIMIT),
        name="outproj_latent" if latent else "outproj_ctx",
    )(o, x, g1, lg, lb, sh, sc, wo, wr)


def _kth_largest_bucket(aff, cap, axes):
    shape = list(aff.shape)
    for a in axes:
        shape[a] = 1
    thr = jnp.zeros(shape, jnp.int32)
    for bit in range(30, -1, -1):
        cand = thr | jnp.int32(1 << bit)
        cnt = _count(aff >= lax.bitcast_convert_type(cand, F32), axes)
        thr = jnp.where(cnt >= cap, cand, thr)
    return lax.bitcast_convert_type(thr, F32), lax.bitcast_convert_type(thr + 1, F32)


def _count(mask, axes):
    c = jnp.where(mask, 1.0, 0.0)
    for a in sorted(axes, reverse=True):
        c = jnp.sum(c, axis=a, keepdims=True)
    return c


def _select_latent_body(a_ref, u_ref, lt_ref, lb_ref, idx_ref, gate_ref, *, cap, nc):
    aff = a_ref[...]
    rows = aff.shape[0]
    aff3 = aff.reshape(N_EXPERTS, nc, LANES)
    t_lo, t_hi = _kth_largest_bucket(aff3, cap, (1, 2))
    gt = aff3 >= t_hi
    eq = (aff3 >= t_lo) & (aff3 < t_hi)
    need = cap - _count(gt, (1, 2))
    eq_f = jnp.where(eq, 1.0, 0.0).reshape(rows, LANES)
    eq_b = eq_f.astype(BF16)
    rank = (_dot(eq_b, u_ref[...]) - eq_f
            + jnp.sum(_dot(lb_ref[...], eq_b), axis=1, keepdims=True))
    sel3 = gt | (eq & (rank.reshape(N_EXPERTS, nc, LANES) < need))
    sel_f = jnp.where(sel3, 1.0, 0.0).reshape(rows, LANES)
    sel_b = sel_f.astype(BF16)
    cnt = jnp.sum(sel_f, axis=1, keepdims=True)
    g_incl = jnp.sum(_dot(lb_ref[...], sel_b), axis=1, keepdims=True) + cnt
    r_row = lax.broadcasted_iota(jnp.int32, (1, cap), 1).astype(F32)
    c_col = lax.broadcasted_iota(jnp.int32, (nc, 1), 0).astype(F32)
    l_col = lax.broadcasted_iota(jnp.int32, (LANES, 1), 0).astype(F32)
    for e in range(N_EXPERTS):
        sl = slice(e * nc, (e + 1) * nc)
        before = g_incl[sl] <= r_row
        chunk = jnp.sum(jnp.where(before, 1.0, 0.0), axis=0, keepdims=True)
        base = jnp.sum(jnp.where(before, cnt[sl], 0.0), axis=0, keepdims=True)
        onehot = jnp.where(c_col == chunk, 1.0, 0.0)
        cs_t = _dot_nt(lt_ref[...], sel_b[sl]).astype(BF16)
        cs_of_r = _dot(cs_t, onehot.astype(BF16))
        local = jnp.sum(jnp.where(cs_of_r <= r_row - base, 1.0, 0.0), axis=0, keepdims=True)
        idx_ref[e:e + 1, :] = (chunk * LANES + local).astype(jnp.int32)
        aff_of_r = jnp.dot(aff[sl].T, onehot, precision=lax.Precision.HIGHEST, preferred_element_type=F32)
        gate_ref[e:e + 1, :] = jnp.sum(jnp.where(l_col == local, aff_of_r, 0.0), axis=0, keepdims=True)


def _select_latent(aff_t, cap):
    bsz, n_exp, seq = aff_t.shape
    nc = seq // LANES
    rows = n_exp * nc
    tri = np.triu(np.ones((LANES, LANES), np.float32))
    r = np.arange(rows)
    blockdiag = ((r[:, None] // nc == r[None, :] // nc) & (r[None, :] < r[:, None])).astype(np.float32)
    const = lambda shape: pl.BlockSpec(shape, lambda b: (0, 0))
    return pl.pallas_call(
        functools.partial(_select_latent_body, cap=cap, nc=nc),
        grid=(bsz,),
        in_specs=[pl.BlockSpec((None, rows, LANES), lambda b: (b, 0, 0)),
                  const((LANES, LANES)), const((LANES, LANES)), const((rows, rows))],
        out_specs=[pl.BlockSpec((None, n_exp, cap), lambda b: (b, 0, 0))] * 2,
        out_shape=[jax.ShapeDtypeStruct((bsz, n_exp, cap), jnp.int32),
                   jax.ShapeDtypeStruct((bsz, n_exp, cap), F32)],
        compiler_params=_cparams(VMEM_LIMIT),
        name="select_latent",
    )(aff_t.reshape(bsz, rows, LANES), jnp.asarray(tri, BF16), jnp.asarray(tri.T, BF16),
      jnp.asarray(blockdiag, BF16))


def _select_ctx_body(a_ref, u_ref, w_ref, *, cap):
    aff = a_ref[...]
    t_lo, t_hi = _kth_largest_bucket(aff, cap, (1,))
    gt = aff >= t_hi
    eq = (aff >= t_lo) & (aff < t_hi)
    need = cap - _count(gt, (1,))
    eq_f = jnp.where(eq, 1.0, 0.0)
    rank = _dot(eq_f.astype(BF16), u_ref[...]) - eq_f
    w_ref[...] = jnp.where(gt | (eq & (rank < need)), aff, 0.0)


def _select_ctx(aff_t, cap):
    bsz, n_exp, lctx = aff_t.shape
    tri = np.triu(np.ones((lctx, lctx), np.float32))
    blk = pl.BlockSpec((None, n_exp, lctx), lambda b: (b, 0, 0))
    return pl.pallas_call(
        functools.partial(_select_ctx_body, cap=cap),
        grid=(bsz,),
        in_specs=[blk, pl.BlockSpec((lctx, lctx), lambda b: (0, 0))],
        out_specs=blk,
        out_shape=jax.ShapeDtypeStruct((bsz, n_exp, lctx), F32),
        name="select_ctx",
    )(aff_t, jnp.asarray(tri, BF16))


def _swiglu(xe, wg_ref, wu_ref, wd_ref):
    a = _dot(xe, wg_ref[...].astype(BF16))
    u = _dot(xe, wu_ref[...].astype(BF16))
    act = (a * jax.nn.sigmoid(a) * u).astype(BF16)
    return _dot(act, wd_ref[...].astype(BF16))


def _expert_gather_body(idx_ref, nxt_ref, hp_ref, wg_ref, wu_ref, wd_ref, y_ref, xa_ref, xb_ref, *, cap):
    e = pl.program_id(1)
    groups = cap // SUBLANES

    def gather_group(rows_ref, dst_ref, g):
        for k in range(SUBLANES):
            dst_ref[g, k:k + 1, :] = hp_ref[pl.ds(rows_ref[0, 0, g * SUBLANES + k], 1), :]

    @pl.when(e == 0)
    def _():
        def body(g, carry):
            gather_group(idx_ref, xa_ref, g)
            return carry
        lax.fori_loop(0, groups, body, 0)

    def ffn_and_prefetch(cur_ref, nxt_buf_ref):
        packed = cur_ref[...].reshape(cap, cur_ref.shape[-1])
        halves = [pltpu.unpack_elementwise(packed, index=i, packed_dtype=BF16, unpacked_dtype=F32).astype(BF16)
                  for i in range(2)]
        xe = jnp.concatenate(halves, axis=1)
        ff, d = wd_ref.shape
        n_pieces = 2 * (ff // MXU_COLS) + d // MXU_COLS
        per_piece = groups // n_pieces
        done = [0]

        def prefetch_slice(last=False):
            stop = groups if last else done[0] + per_piece
            for g in range(done[0], stop):
                gather_group(nxt_ref, nxt_buf_ref, g)
            done[0] = stop

        acts = []
        for j in range(ff // MXU_COLS):
            cols = slice(j * MXU_COLS, (j + 1) * MXU_COLS)
            a = _dot(xe, wg_ref[:, cols].astype(BF16))
            prefetch_slice()
            u = _dot(xe, wu_ref[:, cols].astype(BF16))
            prefetch_slice()
            acts.append((a * jax.nn.sigmoid(a) * u).astype(BF16))
        act = jnp.concatenate(acts, axis=1)
        for j in range(d // MXU_COLS):
            cols = slice(j * MXU_COLS, (j + 1) * MXU_COLS)
            y_ref[:, cols] = _dot(act, wd_ref[:, cols].astype(BF16))
            prefetch_slice(last=j == d // MXU_COLS - 1)

    @pl.when(e % 2 == 0)
    def _():
        ffn_and_prefetch(xa_ref, xb_ref)

    @pl.when(e % 2 == 1)
    def _():
        ffn_and_prefetch(xb_ref, xa_ref)


def _expert_gather(idx, hp, wg, wu, wd, layer):
    bsz, n_exp, cap = idx.shape
    seq, width = hp.shape[1:]
    d, ff = wg.shape[2:]
    assert n_exp % 2 == 0
    rows = pltpu.VMEM((cap // SUBLANES, SUBLANES, width), jnp.uint32)
    return pl.pallas_call(
        functools.partial(_expert_gather_body, cap=cap),
        grid=(bsz, n_exp),
        in_specs=[
            pl.BlockSpec((1, 1, cap), lambda b, e: (b * n_exp + e, 0, 0), memory_space=pltpu.SMEM),
            pl.BlockSpec((1, 1, cap), lambda b, e: (b * n_exp + jnp.minimum(e + 1, n_exp - 1), 0, 0),
                         memory_space=pltpu.SMEM),
            pl.BlockSpec((None, seq, width), lambda b, e: (b, 0, 0), pipeline_mode=pl.Buffered(1)),
            pl.BlockSpec((None, None, d, ff), lambda b, e: (layer, e, 0, 0)),
            pl.BlockSpec((None, None, d, ff), lambda b, e: (layer, e, 0, 0)),
            pl.BlockSpec((None, None, ff, d), lambda b, e: (layer, e, 0, 0)),
        ],
        out_specs=pl.BlockSpec((None, None, cap, d), lambda b, e: (b, e, 0, 0)),
        out_shape=jax.ShapeDtypeStruct((bsz, n_exp, cap, d), F32),
        scratch_shapes=[rows, rows],
        compiler_params=_cparams(VMEM_LIMIT),
        name="expert_gather",
    )(idx.reshape(bsz * n_exp, 1, cap), idx.reshape(bsz * n_exp, 1, cap), hp, wg, wu, wd)


LN_CHUNK = 512


def _scatter_ln_body(idx_ref, gate_ref, y_ref, xm_hbm, g2_ref, lg_ref, lb_ref, o_ref, xbuf, sem, *, cap, alpha):
    b = pl.program_id(0)
    e = pl.program_id(1)
    last = pl.num_programs(1) - 1
    n_chunks = o_ref.shape[0] // LN_CHUNK

    def xm_copy(c, slot):
        return pltpu.make_async_copy(xm_hbm.at[b, pl.ds(c * LN_CHUNK, LN_CHUNK)], xbuf.at[slot], sem.at[slot])

    @pl.when(e == 0)
    def _():
        o_ref[...] = jnp.zeros_like(o_ref)

    @pl.when(e == last)
    def _():
        xm_copy(0, 0).start()

    def add_group(g, carry):
        r0 = g * SUBLANES
        ts = [idx_ref[0, 0, r0 + k] for k in range(SUBLANES)]
        acc = [o_ref[pl.ds(ts[k], 1), :] + gate_ref[0, 0, r0 + k] * y_ref[g, k:k + 1, :]
               for k in range(SUBLANES)]
        for k in range(SUBLANES):
            o_ref[pl.ds(ts[k], 1), :] = acc[k]
        return carry

    lax.fori_loop(0, cap // SUBLANES, add_group, 0)

    @pl.when(e == last)
    def _():
        def ln_chunk(c, carry):
            slot = c % 2
            xm_copy(c, slot).wait()

            @pl.when(c + 1 < n_chunks)
            def _():
                xm_copy(c + 1, 1 - slot).start()

            rows = pl.ds(pl.multiple_of(c * LN_CHUNK, LN_CHUNK), LN_CHUNK)
            z = alpha * xbuf[slot] + g2_ref[...] * o_ref[rows, :]
            o_ref[rows, :] = _ln(z) * lg_ref[...] + lb_ref[...]
            return carry

        lax.fori_loop(0, n_chunks, ln_chunk, 0)


def _scatter_ln(idx, gate, y, xm, g2, lg, lb, alpha):
    bsz, n_exp, cap, d = y.shape
    seq = xm.shape[1]
    assert seq % LN_CHUNK == 0
    sel = lambda b, e: (b * n_exp + e, 0, 0)
    par = pl.BlockSpec((1, d), lambda b, e: (0, 0))
    return pl.pallas_call(
        functools.partial(_scatter_ln_body, cap=cap, alpha=alpha),
        grid=(bsz, n_exp),
        in_specs=[
            pl.BlockSpec((1, 1, cap), sel, memory_space=pltpu.SMEM),
            pl.BlockSpec((1, 1, cap), sel, memory_space=pltpu.SMEM),
            pl.BlockSpec((None, None, cap // SUBLANES, SUBLANES, d), lambda b, e: (b, e, 0, 0, 0)),
            pl.BlockSpec(memory_space=pl.ANY),
            pl.BlockSpec((None, 1, d), lambda b, e: (b, 0, 0)), par, par,
        ],
        out_specs=pl.BlockSpec((None, seq, d), lambda b, e: (b, 0, 0), pipeline_mode=pl.Buffered(1)),
        out_shape=jax.ShapeDtypeStruct((bsz, seq, d), F32),
        scratch_shapes=[pltpu.VMEM((2, LN_CHUNK, d), F32), pltpu.SemaphoreType.DMA((2,))],
        compiler_params=_cparams(VMEM_LIMIT),
        name="scatter_ln",
    )(idx.reshape(bsz * n_exp, 1, cap), gate.reshape(bsz * n_exp, 1, cap),
      y.reshape(bsz, n_exp, cap // SUBLANES, SUBLANES, d), xm, g2, lg, lb)


def _expert_dense_body(h_ref, w_ref, wg_ref, wu_ref, wd_ref, o_ref):
    e = pl.program_id(0)

    @pl.when(e == 0)
    def _():
        o_ref[...] = jnp.zeros_like(o_ref)

    w = w_ref[...]
    lane = lax.broadcasted_iota(jnp.int32, w.shape, 1)
    gate = jnp.sum(jnp.where(lane == e, w, 0.0), axis=1, keepdims=True)
    o_ref[...] += _swiglu(h_ref[...], wg_ref, wu_ref, wd_ref) * gate


def _expert_dense(h, w, wg, wu, wd, layer):
    t, d = h.shape
    n_exp, _, ff = wg.shape[1:]
    return pl.pallas_call(
        _expert_dense_body,
        grid=(n_exp,),
        in_specs=[
            pl.BlockSpec((t, d), lambda e: (0, 0)),
            pl.BlockSpec(w.shape, lambda e: (0, 0)),
            pl.BlockSpec((None, None, d, ff), lambda e: (layer, e, 0, 0)),
            pl.BlockSpec((None, None, d, ff), lambda e: (layer, e, 0, 0)),
            pl.BlockSpec((None, None, ff, d), lambda e: (layer, e, 0, 0)),
        ],
        out_specs=pl.BlockSpec((t, d), lambda e: (0, 0)),
        out_shape=jax.ShapeDtypeStruct((t, d), F32),
        compiler_params=_cparams(VMEM_LIMIT),
        name="expert_dense",
    )(h, w, wg, wu, wd)


def _final_ln_body(xm_ref, f_ref, g2_ref, lg_ref, lb_ref, o_ref, *, alpha):
    z = alpha * xm_ref[...] + g2_ref[...] * f_ref[...]
    o_ref[...] = _ln(z) * lg_ref[...] + lb_ref[...]


def _final_ln(xm, ffn, g2, lg, lb, tm, alpha):
    bsz, seq, d = xm.shape
    tok = pl.BlockSpec((None, tm, d), lambda b, i: (b, i, 0))
    par = pl.BlockSpec((1, d), lambda b, i: (0, 0))
    return pl.pallas_call(
        functools.partial(_final_ln_body, alpha=alpha),
        grid=(bsz, seq // tm),
        in_specs=[tok, tok, pl.BlockSpec((None, 1, d), lambda b, i: (b, 0, 0)), par, par],
        out_specs=tok,
        out_shape=jax.ShapeDtypeStruct((bsz, seq, d), F32),
        compiler_params=_cparams(VMEM_LIMIT),
        name="final_ln",
    )(xm, ffn, g2, lg, lb)


def _rope_tables(seq):
    t = np.arange(seq)
    inv = ROPE_THETA ** (-np.arange(0, ROPE_AXIS_DIM, 2, dtype=np.float32) / ROPE_AXIS_DIM)
    inv = jnp.asarray(inv, F32)

    def axis(pos):
        ang = jnp.asarray(pos, F32)[:, None] * inv[None, :]
        c, s = jnp.cos(ang), jnp.sin(ang)
        return jnp.concatenate([c, c], axis=1), jnp.concatenate([-s, s], axis=1)

    cr, sr = axis(t // GRID_W)
    cc, sc = axis(t % GRID_W)
    reps = LANES // HEAD_DIM
    return (jnp.tile(jnp.concatenate([cr, cc], axis=1), (1, reps)),
            jnp.tile(jnp.concatenate([sr, sc], axis=1), (1, reps)))


def _token_tile(seq):
    return 512 if seq % 512 == 0 else seq


def kernel(x, c, ctx, c_ctx, w_mod, b_mod, w_in, conv_w, attn_sink, na_rpb, w_out, ln1_g, ln1_b,
           w_router, w_gate, w_up, w_down, ln2_g, ln2_b):
    bsz, seq, d = x.shape
    lctx = ctx.shape[1]
    depth = w_mod.shape[0]
    alpha = (2 * depth) ** 0.25
    nb = seq // A_BLOCK
    assert seq % A_BLOCK == 0 and nb >= WIN_BLOCKS + 1 and bsz + 1 <= MOD_ROWS
    tm = _token_tile(seq)
    tmc = _token_tile(bsz * lctx)

    cond = jnp.zeros((MOD_ROWS, d), F32).at[:bsz].set(c).at[bsz].set(c_ctx)
    mod = _modulation(cond, w_mod, b_mod)

    cos, sin = _rope_tables(seq)
    cos_c = jnp.ones((bsz * lctx, LANES), F32)
    sin_c = jnp.zeros((bsz * lctx, LANES), F32)

    head_order = [t * A_GROUP + j for j in range(A_GROUP) for t in range(A_KV_HEADS)]
    a_cols = np.concatenate([np.arange(h * HEAD_DIM, (h + 1) * HEAD_DIM) for h in head_order])
    in_cols = np.concatenate([a_cols, np.arange(A_WIDTH, IN_WIDTH)])
    out_rows = np.concatenate([a_cols, np.arange(A_WIDTH, w_out.shape[1])])

    cap = CAPACITY_FACTOR * seq // N_EXPERTS
    cap_c = CAPACITY_FACTOR * lctx // N_EXPERTS
    ctx_flat = ctx.reshape(1, bsz * lctx, d)

    for l in range(depth):
        last = l == depth - 1
        m6 = mod[l].reshape(MOD_ROWS, N_MOD, d)
        sh1, sc1, g1, sh2, sc2, g2 = (m6[:bsz, i].reshape(bsz, 1, d) for i in range(N_MOD))
        csh1, csc1, cg1, csh2, csc2, cg2 = (m6[bsz:bsz + 1, i].reshape(1, 1, d) for i in range(N_MOD))
        w_in_l = w_in[l][:, in_cols].astype(BF16)
        w_out_l = w_out[l][out_rows, :].astype(BF16)
        w_r = jnp.zeros((d, LANES), BF16).at[:, :N_EXPERTS].set(w_router[l].astype(BF16))
        sink = attn_sink[l]
        lg1, lb1 = ln1_g[l].reshape(1, d), ln1_b[l].reshape(1, d)
        lg2, lb2 = ln2_g[l].reshape(1, d), ln2_b[l].reshape(1, d)

        qa, kva, ub, qn, kvn = _inproj(x, sh1, sc1, cos, sin, w_in_l, tm)
        qa_c, kva_c, ub_c, qn_c, kvn_c = (
            a.reshape(bsz, lctx, a.shape[-1])
            for a in _inproj(ctx_flat, csh1, csc1, cos_c, sin_c, w_in_l, tmc))
        bias = _na_bias_tables(na_rpb[l], nb)
        o = _mix_latent(sink, qa, kva, ub, qn, kvn, kva_c, kvn_c, bias, conv_w[l])
        x_mid, hp, aff_t = _outproj(o, x, g1, lg1, lb1, sh2, sc2, w_out_l, w_r, tm, alpha, True)

        if not last:
            o_c = _mix_ctx(sink, qa_c, kva_c, ub_c, qn_c, kvn_c, conv_w[l])
            ctx_mid, h2c, aff_ct = _outproj(
                o_c.reshape(1, bsz * lctx, d), ctx_flat, cg1, lg1, lb1, csh2, csc2, w_out_l, w_r,
                tmc, alpha, False)
            aff_ct = jnp.transpose(aff_ct.reshape(N_EXPERTS, bsz, lctx), (1, 0, 2))
            gate_c = _select_ctx(aff_ct, cap_c)
            gate_c = jnp.transpose(gate_c, (0, 2, 1)).reshape(bsz * lctx, N_EXPERTS)
            ffn_c = _expert_dense(h2c[0], gate_c, w_gate, w_up, w_down, l)
            ctx_flat = _final_ln(ctx_mid, ffn_c[None], cg2, lg2, lb2, tmc, alpha)

        idx, gate = _select_latent(aff_t, cap)
        y = _expert_gather(idx, hp, w_gate, w_up, w_down, l)
        x = _scatter_ln(idx, gate, y, x_mid, g2, lg2, lb2, alpha)
    return x
```

```python
import functools

import numpy as np
import jax
import jax.numpy as jnp
from jax import lax
from jax.experimental import pallas as pl
from jax.experimental.pallas import tpu as pltpu

HEAD_DIM = 64
GRID_W = 64
A_Q_HEADS = 6
A_KV_HEADS = 2
A_GROUP = A_Q_HEADS // A_KV_HEADS
A_WINDOW = 128
A_BLOCK = 128
B_WIDTH = 256
C_HEADS = 6
NA_ROWS = 8
NA_COLS = 16
A_WIDTH = A_Q_HEADS * HEAD_DIM
A_KV_WIDTH = A_KV_HEADS * HEAD_DIM
C_WIDTH = C_HEADS * HEAD_DIM
N_EXPERTS = 16
CAPACITY_FACTOR = 2
ROPE_THETA = 10000.0
ROPE_AXIS_DIM = HEAD_DIM // 2
LN_EPS = 1e-6
N_MOD = 6
NEG_INF = -1e30

LANES = 128
SUBLANES = 8
MXU_COLS = 256
MOD_ROWS = SUBLANES
HALO_ROWS = 16
WIN_BLOCKS = 5
N_BIAS_PATTERNS = 5
VMEM_LIMIT = 56 * 1024 * 1024

BF16 = jnp.bfloat16
F32 = jnp.float32


def _cparams(vmem=None):
    return pltpu.CompilerParams(vmem_limit_bytes=vmem) if vmem else None


def _dot(a, b):
    return jnp.dot(a, b, preferred_element_type=F32)


def _dot_nt(a, b):
    return lax.dot_general(a, b, (((1,), (1,)), ((), ())), preferred_element_type=F32)


def _ln(x):
    mu = jnp.mean(x, axis=-1, keepdims=True)
    xc = x - mu
    var = jnp.mean(xc * xc, axis=-1, keepdims=True)
    return xc * lax.rsqrt(var + LN_EPS)


def _mod_body(c_ref, w_ref, b_ref, o_ref):
    c = c_ref[...]
    h = (c * jax.nn.sigmoid(c)).astype(BF16)
    o_ref[...] = _dot(h, w_ref[...].astype(BF16)) + b_ref[...]


def _modulation(cond, w_mod, b_mod):
    depth, d, n = w_mod.shape
    tn = n // 4
    return pl.pallas_call(
        _mod_body,
        grid=(depth, n // tn),
        in_specs=[
            pl.BlockSpec((MOD_ROWS, d), lambda l, j: (0, 0)),
            pl.BlockSpec((None, d, tn), lambda l, j: (l, 0, j)),
            pl.BlockSpec((None, 1, tn), lambda l, j: (l, 0, j)),
        ],
        out_specs=pl.BlockSpec((None, MOD_ROWS, tn), lambda l, j: (l, 0, j)),
        out_shape=jax.ShapeDtypeStruct((depth, MOD_ROWS, n), F32),
        compiler_params=_cparams(VMEM_LIMIT),
        name="modulation",
    )(cond, w_mod, b_mod.reshape(depth, 1, n))


QK_W = A_WIDTH + HEAD_DIM * A_KV_HEADS
O_VA = QK_W
O_B = O_VA + A_KV_WIDTH
O_QN = O_B + 3 * B_WIDTH
O_KVN = O_QN + C_WIDTH
IN_WIDTH = O_KVN + 2 * C_WIDTH


def _inproj_body(x_ref, sh_ref, sc_ref, cos_ref, sin_ref, w_ref,
                 qa_ref, kva_ref, ub_ref, qn_ref, kvn_ref):
    h = _ln(x_ref[...]) * (1.0 + sc_ref[...]) + sh_ref[...]
    hb = h.astype(BF16)
    scale = HEAD_DIM ** -0.5
    qk = _dot(hb, w_ref[:, 0:QK_W])
    cos = cos_ref[...]
    sin = sin_ref[...]
    lane = lax.broadcasted_iota(jnp.int32, cos.shape, 1)
    first = (lane % ROPE_AXIS_DIM) < (ROPE_AXIS_DIM // 2)
    half = ROPE_AXIS_DIM // 2
    parts = []
    for j in range(QK_W // LANES):
        t = qk[:, j * LANES:(j + 1) * LANES]
        partner = jnp.where(first, pltpu.roll(t, LANES - half, 1), pltpu.roll(t, half, 1))
        parts.append(t * cos + partner * sin)
    qa_ref[...] = (jnp.concatenate(parts[:-1], axis=1) * scale).astype(BF16)
    va = _dot(hb, w_ref[:, O_VA:O_B])
    kva_ref[...] = jnp.concatenate([parts[-1], va], axis=1).astype(BF16)
    g = _dot(hb, w_ref[:, O_B:O_QN])
    bx, bb, bc = g[:, :B_WIDTH], g[:, B_WIDTH:2 * B_WIDTH], g[:, 2 * B_WIDTH:]
    ub_ref[...] = jnp.concatenate([bc * bx, bb], axis=1).astype(BF16)
    qn_ref[...] = (_dot(hb, w_ref[:, O_QN:O_KVN]) * scale).astype(BF16)
    kvn_ref[...] = _dot(hb, w_ref[:, O_KVN:IN_WIDTH]).astype(BF16)


def _inproj(x, sh, sc, cos, sin, w, tm):
    bsz, seq, d = x.shape
    tok = lambda width: pl.BlockSpec((None, tm, width), lambda b, i: (b, i, 0))
    vec = pl.BlockSpec((None, 1, d), lambda b, i: (b, 0, 0))
    tab = pl.BlockSpec((tm, LANES), lambda b, i: (i, 0))
    widths = (A_WIDTH, 2 * A_KV_WIDTH, 2 * B_WIDTH, C_WIDTH, 2 * C_WIDTH)
    return pl.pallas_call(
        _inproj_body,
        grid=(bsz, seq // tm),
        in_specs=[tok(d), vec, vec, tab, tab, pl.BlockSpec((d, IN_WIDTH), lambda b, i: (0, 0))],
        out_specs=[tok(wd) for wd in widths],
        out_shape=[jax.ShapeDtypeStruct((bsz, seq, wd), BF16) for wd in widths],
        compiler_params=_cparams(VMEM_LIMIT),
        name="inproj",
    )(x, sh, sc, cos, sin, w)


def _split_heads(q, lo):
    zero = jnp.zeros_like(q)
    return jnp.concatenate([jnp.where(lo, q, zero), jnp.where(lo, zero, q)], axis=0)


def _with_ones(v_cat):
    return jnp.concatenate([v_cat, jnp.ones_like(v_cat)], axis=1)


def _attend(units, m_rows):
    lo = lax.broadcasted_iota(jnp.int32, (m_rows, LANES), 1) < HEAD_DIM
    scores = lambda unit: _dot_nt(_split_heads(unit[0], lo), unit[1])
    s_next = scores(units[0])
    outs = []
    for i, (_, _, values, probs) in enumerate(units):
        s = s_next
        if i + 1 < len(units):
            s_next = scores(units[i + 1])
        e, extra = probs(s)
        o = _dot(e, values)
        den = o[:, LANES:] if extra is None else o[:, LANES:] + extra
        o = o[:, :LANES] / den
        outs.append(jnp.where(lo, o[:m_rows], o[m_rows:]))
    return outs


def _gqa_probs(sink_ref, mask, tile, m_rows):
    def probs(s):
        es, sinks = [], []
        for half in range(A_KV_HEADS):
            sb = s[half * m_rows:(half + 1) * m_rows]
            if mask is not None:
                sb = jnp.where(mask, sb, NEG_INF)
            sk = sink_ref[half * A_GROUP + tile]
            m = jnp.maximum(jnp.max(sb, axis=1, keepdims=True), sk)
            es.append(jnp.exp((sb - m).astype(BF16)))
            sinks.append(jnp.exp(sk - m))
        return jnp.concatenate(es, axis=0), jnp.concatenate(sinks, axis=0)
    return probs


def _mha_probs(bias_pair, n_biased):
    def probs(s):
        if bias_pair is None:
            return jnp.exp((s - jnp.max(s, axis=1, keepdims=True)).astype(BF16)), None
        s_w = s[:, :n_biased] + bias_pair
        s_c = s[:, n_biased:]
        m = jnp.maximum(jnp.max(s_w, axis=1, keepdims=True), jnp.max(s_c, axis=1, keepdims=True))
        return jnp.concatenate([jnp.exp((s_w - m).astype(BF16)), jnp.exp((s_c - m).astype(BF16))], axis=1), None
    return probs


def _short_conv(ub, prev_row, next_row, w_ref, m_rows):
    u = ub[:, :B_WIDTH].astype(F32)
    bb = ub[:, B_WIDTH:].astype(F32)
    row = lax.broadcasted_iota(jnp.int32, u.shape, 0)
    u_m1 = jnp.where(row == 0, prev_row, pltpu.roll(u, 1, 0))
    u_p1 = jnp.where(row == m_rows - 1, next_row, pltpu.roll(u, m_rows - 1, 0))
    y = u_m1 * w_ref[0:1, :] + u * w_ref[1:2, :] + u_p1 * w_ref[2:3, :]
    return bb * y


def _mix_latent_body(sink_ref, qa_ref, kp_ref, ko_ref, kn_ref, ubp_ref, ubo_ref, ubn_ref, qn_ref,
                     w0_ref, w1_ref, w2_ref, w3_ref, w4_ref, kvac_ref, kvnc_ref, bias_ref, cw_ref,
                     o_ref, *, seq):
    n = pl.program_id(1)
    nb = pl.num_programs(1)
    m_rows = A_BLOCK
    k_cat = jnp.concatenate([kp_ref[:, :A_KV_WIDTH], ko_ref[:, :A_KV_WIDTH], kn_ref[:, :A_KV_WIDTH],
                             kvac_ref[:, :A_KV_WIDTH]], axis=0)
    v_cat = jnp.concatenate([kp_ref[:, A_KV_WIDTH:], ko_ref[:, A_KV_WIDTH:], kn_ref[:, A_KV_WIDTH:],
                             kvac_ref[:, A_KV_WIDTH:]], axis=0)
    n_keys = k_cat.shape[0]
    qpos = n * A_BLOCK + lax.broadcasted_iota(jnp.int32, (m_rows, n_keys), 0)
    col = lax.broadcasted_iota(jnp.int32, (m_rows, n_keys), 1)
    kpos = (n - 1) * A_BLOCK + col
    mask = (col >= 3 * A_BLOCK) | ((jnp.abs(qpos - kpos) <= A_WINDOW) & (kpos >= 0) & (kpos < seq))
    v_ones = _with_ones(v_cat)
    units = [(qa_ref[:, j * LANES:(j + 1) * LANES], k_cat, v_ones, _gqa_probs(sink_ref, mask, j, m_rows))
             for j in range(A_WIDTH // LANES)]
    wins = (w0_ref, w1_ref, w2_ref, w3_ref, w4_ref)
    n_win = WIN_BLOCKS * A_BLOCK
    for j in range(C_WIDTH // LANES):
        ks = slice(j * LANES, (j + 1) * LANES)
        vs = slice(C_WIDTH + j * LANES, C_WIDTH + (j + 1) * LANES)
        kc = jnp.concatenate([w[:, ks] for w in wins] + [kvnc_ref[:, ks]], axis=0)
        vc = jnp.concatenate([w[:, vs] for w in wins] + [kvnc_ref[:, vs]], axis=0)
        bias_pair = jnp.concatenate([bias_ref[2 * j], bias_ref[2 * j + 1]], axis=0)
        units.append((qn_ref[:, ks], kc, _with_ones(vc), _mha_probs(bias_pair, n_win)))
    outs = _attend(units, m_rows)
    prev_row = jnp.where(n > 0, ubp_ref[HALO_ROWS - 1:HALO_ROWS, :B_WIDTH].astype(F32), 0.0)
    next_row = jnp.where(n < nb - 1, ubn_ref[0:1, :B_WIDTH].astype(F32), 0.0)
    o_b = _short_conv(ubo_ref[...], prev_row, next_row, cw_ref, m_rows)
    n_a = A_WIDTH // LANES
    o_ref[...] = jnp.concatenate(outs[:n_a] + [o_b] + outs[n_a:], axis=1).astype(BF16)


def _mix_latent(sink, qa, kva, ub, qn, kvn, kva_c, kvn_c, bias, conv_w):
    bsz, seq, _ = qa.shape
    nb = seq // A_BLOCK
    lctx = kva_c.shape[1]
    blk = A_BLOCK
    hb = blk // HALO_ROWS
    n_halo = seq // HALO_ROWS
    tok = lambda width, fn: pl.BlockSpec((None, blk, width), fn)
    own = lambda b, n: (b, n, 0)
    win = lambda i: (lambda b, n: (b, jnp.clip(n - 2, 0, nb - WIN_BLOCKS) + i, 0))
    pattern = lambda b, n: (jnp.minimum(n, 2) + jnp.maximum(n - (nb - 3), 0), 0, 0, 0)
    in_specs = [
        pl.BlockSpec(memory_space=pltpu.SMEM),
        tok(A_WIDTH, own),
        tok(2 * A_KV_WIDTH, lambda b, n: (b, jnp.maximum(n - 1, 0), 0)),
        tok(2 * A_KV_WIDTH, own),
        tok(2 * A_KV_WIDTH, lambda b, n: (b, jnp.minimum(n + 1, nb - 1), 0)),
        pl.BlockSpec((None, HALO_ROWS, 2 * B_WIDTH), lambda b, n: (b, jnp.maximum(n * hb - 1, 0), 0)),
        tok(2 * B_WIDTH, own),
        pl.BlockSpec((None, HALO_ROWS, 2 * B_WIDTH), lambda b, n: (b, jnp.minimum((n + 1) * hb, n_halo - 1), 0)),
        tok(C_WIDTH, own),
    ] + [tok(2 * C_WIDTH, win(i)) for i in range(WIN_BLOCKS)] + [
        pl.BlockSpec((None, lctx, 2 * A_KV_WIDTH), lambda b, n: (b, 0, 0)),
        pl.BlockSpec((None, lctx, 2 * C_WIDTH), lambda b, n: (b, 0, 0)),
        pl.BlockSpec((None, C_HEADS, blk, WIN_BLOCKS * blk), pattern),
        pl.BlockSpec(conv_w.shape, lambda b, n: (0, 0)),
    ]
    return pl.pallas_call(
        functools.partial(_mix_latent_body, seq=seq),
        grid=(bsz, nb),
        in_specs=in_specs,
        out_specs=tok(A_WIDTH + B_WIDTH + C_WIDTH, own),
        out_shape=jax.ShapeDtypeStruct((bsz, seq, A_WIDTH + B_WIDTH + C_WIDTH), BF16),
        compiler_params=_cparams(VMEM_LIMIT),
        name="mix_latent",
    )(sink, qa, kva, kva, kva, ub, ub, ub, qn, kvn, kvn, kvn, kvn, kvn, kva_c, kvn_c, bias, conv_w)


def _mix_ctx_body(sink_ref, qa_ref, kva_ref, ub_ref, qn_ref, kvn_ref, cw_ref, o_ref):
    m_rows = qa_ref.shape[0]
    v_ones = _with_ones(kva_ref[:, A_KV_WIDTH:])
    units = [(qa_ref[:, j * LANES:(j + 1) * LANES], kva_ref[:, :A_KV_WIDTH], v_ones,
              _gqa_probs(sink_ref, None, j, m_rows)) for j in range(A_WIDTH // LANES)]
    for j in range(C_WIDTH // LANES):
        ks = slice(j * LANES, (j + 1) * LANES)
        vs = slice(C_WIDTH + j * LANES, C_WIDTH + (j + 1) * LANES)
        units.append((qn_ref[:, ks], kvn_ref[:, ks], _with_ones(kvn_ref[:, vs]), _mha_probs(None, 0)))
    outs = _attend(units, m_rows)
    o_b = _short_conv(ub_ref[...], 0.0, 0.0, cw_ref, m_rows)
    n_a = A_WIDTH // LANES
    o_ref[...] = jnp.concatenate(outs[:n_a] + [o_b] + outs[n_a:], axis=1).astype(BF16)


def _mix_ctx(sink, qa, kva, ub, qn, kvn, conv_w):
    bsz, lctx, _ = qa.shape
    full = lambda width: pl.BlockSpec((None, lctx, width), lambda b: (b, 0, 0))
    width = A_WIDTH + B_WIDTH + C_WIDTH
    return pl.pallas_call(
        _mix_ctx_body,
        grid=(bsz,),
        in_specs=[pl.BlockSpec(memory_space=pltpu.SMEM), full(A_WIDTH), full(2 * A_KV_WIDTH),
                  full(2 * B_WIDTH), full(C_WIDTH), full(2 * C_WIDTH),
                  pl.BlockSpec(conv_w.shape, lambda b: (0, 0))],
        out_specs=full(width),
        out_shape=jax.ShapeDtypeStruct((bsz, lctx, width), BF16),
        compiler_params=_cparams(VMEM_LIMIT),
        name="mix_ctx",
    )(sink, qa, kva, ub, qn, kvn, conv_w)


def _na_bias_tables(rpb, nb):
    per_blk = A_BLOCK // GRID_W
    rows = nb * per_blk
    kh = min(NA_ROWS, rows)
    qc = np.arange(GRID_W)[:, None]
    kc = np.arange(GRID_W)[None, :]
    coff = np.clip(kc - qc, -(NA_COLS - 1), NA_COLS - 1) + (NA_COLS - 1)
    cs = np.clip(qc - NA_COLS // 2, 0, GRID_W - NA_COLS)
    col_ok = (kc >= cs) & (kc < cs + NA_COLS)
    onehot = (coff[None] == np.arange(2 * NA_COLS - 1)[:, None, None]).astype(np.float32)
    tiles = jnp.einsum('hrc,cqk->hrqk', rpb, onehot, precision=lax.Precision.HIGHEST)
    tiles = jnp.where(col_ok, tiles, NEG_INF)
    masked = jnp.full((rpb.shape[0], GRID_W, GRID_W), NEG_INF, F32)
    patterns = []
    for blk in (0, 1, 2, nb - 2, nb - 1):
        wb = min(max(blk - 2, 0), nb - WIN_BLOCKS)
        q_rows = []
        for qi in range(per_blk):
            qr = per_blk * blk + qi
            rs = min(max(qr - kh // 2, 0), rows - kh)
            k_tiles = []
            for kj in range(WIN_BLOCKS * per_blk):
                kr = per_blk * wb + kj
                k_tiles.append(tiles[:, kr - qr + NA_ROWS - 1] if rs <= kr < rs + kh else masked)
            q_rows.append(jnp.concatenate(k_tiles, axis=-1))
        patterns.append(jnp.concatenate(q_rows, axis=-2))
    return jnp.stack(patterns)


def _router_affinity(h2b, wr_ref):
    logits = _dot(h2b, wr_ref[...])
    lane = lax.broadcasted_iota(jnp.int32, logits.shape, 1)
    logits = jnp.where(lane < N_EXPERTS, logits, NEG_INF)
    e = jnp.exp(logits - jnp.max(logits, axis=1, keepdims=True))
    return e / jnp.sum(e, axis=1, keepdims=True)


def _outproj_core(o_ref, x_ref, g1_ref, lg_ref, lb_ref, sh_ref, sc_ref, wo_ref, wr_ref, alpha):
    mix = _dot(o_ref[...], wo_ref[...])
    z = alpha * x_ref[...] + g1_ref[...] * mix
    x_mid = _ln(z) * lg_ref[...] + lb_ref[...]
    h2 = _ln(x_mid) * (1.0 + sc_ref[...]) + sh_ref[...]
    return x_mid, h2, _router_affinity(h2.astype(BF16), wr_ref)


def _outproj_body(o_ref, x_ref, g1_ref, lg_ref, lb_ref, sh_ref, sc_ref, wo_ref, wr_ref,
                  xm_ref, h2_ref, afft_ref, *, alpha, packed):
    x_mid, h2, aff = _outproj_core(o_ref, x_ref, g1_ref, lg_ref, lb_ref, sh_ref, sc_ref, wo_ref, wr_ref, alpha)
    xm_ref[...] = x_mid
    if packed:
        half = h2.shape[1] // 2
        h2_ref[...] = pltpu.pack_elementwise([h2[:, :half], h2[:, half:]], packed_dtype=BF16)
    else:
        h2_ref[...] = h2.astype(BF16)
    afft_ref[...] = aff.T[:N_EXPERTS, :]


def _outproj(o, x, g1, lg, lb, sh, sc, wo, wr, tm, alpha, latent):
    bsz, seq, d = x.shape
    tok = lambda width: pl.BlockSpec((None, tm, width), lambda b, i: (b, i, 0))
    vec = pl.BlockSpec((None, 1, d), lambda b, i: (b, 0, 0))
    par = pl.BlockSpec((1, d), lambda b, i: (0, 0))
    afft_spec = pl.BlockSpec((None, N_EXPERTS, tm), lambda b, i: (b, 0, i))
    afft_shape = jax.ShapeDtypeStruct((bsz, N_EXPERTS, seq), F32)
    h2_width, h2_dtype = (d // 2, jnp.uint32) if latent else (d, BF16)
    out_specs = [tok(d), tok(h2_width), afft_spec]
    out_shape = [jax.ShapeDtypeStruct((bsz, seq, d), F32), jax.ShapeDtypeStruct((bsz, seq, h2_width), h2_dtype),
                 afft_shape]
    return pl.pallas_call(
        functools.partial(_outproj_body, alpha=alpha, packed=latent),
        grid=(bsz, seq // tm),
        in_specs=[tok(d), tok(d), vec, par, par, vec, vec,
                  pl.BlockSpec(wo.shape, lambda b, i: (0, 0)), pl.BlockSpec(wr.shape, lambda b, i: (0, 0))],
        out_specs=out_specs,
        out_shape=out_shape,
        compiler_params=_cparams(VMEM_LIMIT),
        name="outproj_latent" if latent else "outproj_ctx",
    )(o, x, g1, lg, lb, sh, sc, wo, wr)


def _kth_largest_bucket(aff, cap, axes):
    shape = list(aff.shape)
    for a in axes:
        shape[a] = 1
    thr = jnp.zeros(shape, jnp.int32)
    for bit in range(30, -1, -1):
        cand = thr | jnp.int32(1 << bit)
        cnt = _count(aff >= lax.bitcast_convert_type(cand, F32), axes)
        thr = jnp.where(cnt >= cap, cand, thr)
    return lax.bitcast_convert_type(thr, F32), lax.bitcast_convert_type(thr + 1, F32)


def _count(mask, axes):
    c = jnp.where(mask, 1.0, 0.0)
    for a in sorted(axes):
        c = jnp.sum(c, axis=a, keepdims=True)
    return c


def _select_latent_body(a_ref, u_ref, lt_ref, lb_ref, idx_ref, gate_ref, *, cap, nc):
    aff = a_ref[...]
    rows = aff.shape[0]
    aff3 = aff.reshape(N_EXPERTS, nc, LANES)
    t_lo, t_hi = _kth_largest_bucket(aff3, cap, (1, 2))
    gt = aff3 >= t_hi
    eq = (aff3 >= t_lo) & (aff3 < t_hi)
    need = cap - _count(gt, (1, 2))
    eq_f = jnp.where(eq, 1.0, 0.0).reshape(rows, LANES)
    eq_b = eq_f.astype(BF16)
    rank = (_dot(eq_b, u_ref[...]) - eq_f
            + jnp.sum(_dot(lb_ref[...], eq_b), axis=1, keepdims=True))
    sel3 = gt | (eq & (rank.reshape(N_EXPERTS, nc, LANES) < need))
    sel_f = jnp.where(sel3, 1.0, 0.0).reshape(rows, LANES)
    sel_b = sel_f.astype(BF16)
    cnt = jnp.sum(sel_f, axis=1, keepdims=True)
    g_incl = jnp.sum(_dot(lb_ref[...], sel_b), axis=1, keepdims=True) + cnt
    r_row = lax.broadcasted_iota(jnp.int32, (1, cap), 1).astype(F32)
    c_col = lax.broadcasted_iota(jnp.int32, (nc, 1), 0).astype(F32)
    l_col = lax.broadcasted_iota(jnp.int32, (LANES, 1), 0).astype(F32)
    for e in range(N_EXPERTS):
        sl = slice(e * nc, (e + 1) * nc)
        before = g_incl[sl] <= r_row
        chunk = jnp.sum(jnp.where(before, 1.0, 0.0), axis=0, keepdims=True)
        base = jnp.sum(jnp.where(before, cnt[sl], 0.0), axis=0, keepdims=True)
        onehot = jnp.where(c_col == chunk, 1.0, 0.0)
        cs_t = _dot_nt(lt_ref[...], sel_b[sl]).astype(BF16)
        cs_of_r = _dot(cs_t, onehot.astype(BF16))
        local = jnp.sum(jnp.where(cs_of_r <= r_row - base, 1.0, 0.0), axis=0, keepdims=True)
        idx_ref[e:e + 1, :] = (chunk * LANES + local).astype(jnp.int32)
        aff_of_r = jnp.dot(aff[sl].T, onehot, precision=lax.Precision.HIGHEST, preferred_element_type=F32)
        gate_ref[e:e + 1, :] = jnp.sum(jnp.where(l_col == local, aff_of_r, 0.0), axis=0, keepdims=True)


def _select_latent(aff_t, cap):
    bsz, n_exp, seq = aff_t.shape
    nc = seq // LANES
    rows = n_exp * nc
    tri = np.triu(np.ones((LANES, LANES), np.float32))
    r = np.arange(rows)
    blockdiag = ((r[:, None] // nc == r[None, :] // nc) & (r[None, :] < r[:, None])).astype(np.float32)
    const = lambda shape: pl.BlockSpec(shape, lambda b: (0, 0))
    return pl.pallas_call(
        functools.partial(_select_latent_body, cap=cap, nc=nc),
        grid=(bsz,),
        in_specs=[pl.BlockSpec((None, rows, LANES), lambda b: (b, 0, 0)),
                  const((LANES, LANES)), const((LANES, LANES)), const((rows, rows))],
        out_specs=[pl.BlockSpec((None, n_exp, cap), lambda b: (b, 0, 0))] * 2,
        out_shape=[jax.ShapeDtypeStruct((bsz, n_exp, cap), jnp.int32),
                   jax.ShapeDtypeStruct((bsz, n_exp, cap), F32)],
        compiler_params=_cparams(VMEM_LIMIT),
        name="select_latent",
    )(aff_t.reshape(bsz, rows, LANES), jnp.asarray(tri, BF16), jnp.asarray(tri.T, BF16),
      jnp.asarray(blockdiag, BF16))


def _select_ctx_body(a_ref, u_ref, w_ref, *, cap):
    aff = a_ref[...]
    t_lo, t_hi = _kth_largest_bucket(aff, cap, (1,))
    gt = aff >= t_hi
    eq = (aff >= t_lo) & (aff < t_hi)
    need = cap - _count(gt, (1,))
    eq_f = jnp.where(eq, 1.0, 0.0)
    rank = _dot(eq_f.astype(BF16), u_ref[...]) - eq_f
    w_ref[...] = jnp.where(gt | (eq & (rank < need)), aff, 0.0)


def _select_ctx(aff_t, cap):
    bsz, n_exp, lctx = aff_t.shape
    tri = np.triu(np.ones((lctx, lctx), np.float32))
    blk = pl.BlockSpec((None, n_exp, lctx), lambda b: (b, 0, 0))
    return pl.pallas_call(
        functools.partial(_select_ctx_body, cap=cap),
        grid=(bsz,),
        in_specs=[blk, pl.BlockSpec((lctx, lctx), lambda b: (0, 0))],
        out_specs=blk,
        out_shape=jax.ShapeDtypeStruct((bsz, n_exp, lctx), F32),
        name="select_ctx",
    )(aff_t, jnp.asarray(tri, BF16))


def _swiglu(xe, wg_ref, wu_ref, wd_ref):
    a = _dot(xe, wg_ref[...].astype(BF16))
    u = _dot(xe, wu_ref[...].astype(BF16))
    act = (a * jax.nn.sigmoid(a) * u).astype(BF16)
    return _dot(act, wd_ref[...].astype(BF16))


def _expert_gather_body(idx_ref, nxt_ref, hp_ref, wg_ref, wu_ref, wd_ref, y_ref, xa_ref, xb_ref, *, cap):
    e = pl.program_id(1)
    groups = cap // SUBLANES

    def gather_group(rows_ref, dst_ref, g):
        for k in range(SUBLANES):
            dst_ref[g, k:k + 1, :] = hp_ref[pl.ds(rows_ref[0, 0, g * SUBLANES + k], 1), :]

    @pl.when(e == 0)
    def _():
        def body(g, carry):
            gather_group(idx_ref, xa_ref, g)
            return carry
        lax.fori_loop(0, groups, body, 0)

    def ffn_and_prefetch(cur_ref, nxt_buf_ref):
        packed = cur_ref[...].reshape(cap, cur_ref.shape[-1])
        halves = [pltpu.unpack_elementwise(packed, index=i, packed_dtype=BF16, unpacked_dtype=F32).astype(BF16)
                  for i in range(2)]
        xe = jnp.concatenate(halves, axis=1)
        ff, d = wd_ref.shape
        n_pieces = 2 * (ff // MXU_COLS) + d // MXU_COLS
        per_piece = groups // n_pieces
        done = [0]

        def prefetch_slice(last=False):
            stop = groups if last else done[0] + per_piece
            for g in range(done[0], stop):
                gather_group(nxt_ref, nxt_buf_ref, g)
            done[0] = stop

        acts = []
        for j in range(ff // MXU_COLS):
            cols = slice(j * MXU_COLS, (j + 1) * MXU_COLS)
            a = _dot(xe, wg_ref[:, cols].astype(BF16))
            prefetch_slice()
            u = _dot(xe, wu_ref[:, cols].astype(BF16))
            prefetch_slice()
            acts.append((a * jax.nn.sigmoid(a) * u).astype(BF16))
        act = jnp.concatenate(acts, axis=1)
        for j in range(d // MXU_COLS):
            cols = slice(j * MXU_COLS, (j + 1) * MXU_COLS)
            y_ref[:, cols] = _dot(act, wd_ref[:, cols].astype(BF16))
            prefetch_slice(last=j == d // MXU_COLS - 1)

    @pl.when(e % 2 == 0)
    def _():
        ffn_and_prefetch(xa_ref, xb_ref)

    @pl.when(e % 2 == 1)
    def _():
        ffn_and_prefetch(xb_ref, xa_ref)


def _expert_gather(idx, hp, wg, wu, wd, layer):
    bsz, n_exp, cap = idx.shape
    seq, width = hp.shape[1:]
    d, ff = wg.shape[2:]
    assert n_exp % 2 == 0
    rows = pltpu.VMEM((cap // SUBLANES, SUBLANES, width), jnp.uint32)
    return pl.pallas_call(
        functools.partial(_expert_gather_body, cap=cap),
        grid=(bsz, n_exp),
        in_specs=[
            pl.BlockSpec((1, 1, cap), lambda b, e: (b * n_exp + e, 0, 0), memory_space=pltpu.SMEM),
            pl.BlockSpec((1, 1, cap), lambda b, e: (b * n_exp + jnp.minimum(e + 1, n_exp - 1), 0, 0),
                         memory_space=pltpu.SMEM),
            pl.BlockSpec((None, seq, width), lambda b, e: (b, 0, 0), pipeline_mode=pl.Buffered(1)),
            pl.BlockSpec((None, None, d, ff), lambda b, e: (layer, e, 0, 0)),
            pl.BlockSpec((None, None, d, ff), lambda b, e: (layer, e, 0, 0)),
            pl.BlockSpec((None, None, ff, d), lambda b, e: (layer, e, 0, 0)),
        ],
        out_specs=pl.BlockSpec((None, None, cap, d), lambda b, e: (b, e, 0, 0)),
        out_shape=jax.ShapeDtypeStruct((bsz, n_exp, cap, d), F32),
        scratch_shapes=[rows, rows],
        compiler_params=_cparams(VMEM_LIMIT),
        name="expert_gather",
    )(idx.reshape(bsz * n_exp, 1, cap), idx.reshape(bsz * n_exp, 1, cap), hp, wg, wu, wd)


LN_CHUNK = 512


def _scatter_ln_body(idx_ref, gate_ref, y_ref, xm_hbm, g2_ref, lg_ref, lb_ref, out_hbm, o_ref, xbuf, obuf,
                     sem_in, sem_out, *, cap, alpha):
    b = pl.program_id(0)
    e = pl.program_id(1)
    last = pl.num_programs(1) - 1
    n_chunks = o_ref.shape[0] // LN_CHUNK

    def xm_copy(c, slot):
        return pltpu.make_async_copy(xm_hbm.at[b, pl.ds(c * LN_CHUNK, LN_CHUNK)], xbuf.at[slot], sem_in.at[slot])

    def out_copy(c, slot):
        return pltpu.make_async_copy(obuf.at[slot], out_hbm.at[b, pl.ds(c * LN_CHUNK, LN_CHUNK)], sem_out.at[slot])

    @pl.when(e == 0)
    def _():
        o_ref[...] = jnp.zeros_like(o_ref)

    @pl.when(e == last)
    def _():
        xm_copy(0, 0).start()

    def add_group(g, carry):
        r0 = g * SUBLANES
        ts = [idx_ref[0, 0, r0 + k] for k in range(SUBLANES)]
        acc = [o_ref[pl.ds(ts[k], 1), :] + gate_ref[0, 0, r0 + k] * y_ref[g, k:k + 1, :]
               for k in range(SUBLANES)]
        for k in range(SUBLANES):
            o_ref[pl.ds(ts[k], 1), :] = acc[k]
        return carry

    lax.fori_loop(0, cap // SUBLANES, add_group, 0)

    @pl.when(e == last)
    def _():
        def ln_chunk(c, carry):
            slot = c % 2
            xm_copy(c, slot).wait()

            @pl.when(c + 1 < n_chunks)
            def _():
                xm_copy(c + 1, 1 - slot).start()

            @pl.when(c >= 2)
            def _():
                out_copy(c - 2, slot).wait()

            rows = pl.ds(pl.multiple_of(c * LN_CHUNK, LN_CHUNK), LN_CHUNK)
            z = alpha * xbuf[slot] + g2_ref[...] * o_ref[rows, :]
            obuf[slot] = _ln(z) * lg_ref[...] + lb_ref[...]
            out_copy(c, slot).start()
            return carry

        lax.fori_loop(0, n_chunks, ln_chunk, 0)
        for c in (n_chunks - 2, n_chunks - 1):
            out_copy(c, c % 2).wait()


def _scatter_ln(idx, gate, y, xm, g2, lg, lb, alpha):
    bsz, n_exp, cap, d = y.shape
    seq = xm.shape[1]
    assert seq % LN_CHUNK == 0 and seq // LN_CHUNK >= 2
    sel = lambda b, e: (b * n_exp + e, 0, 0)
    par = pl.BlockSpec((1, d), lambda b, e: (0, 0))
    chunk_pair = pltpu.VMEM((2, LN_CHUNK, d), F32)
    return pl.pallas_call(
        functools.partial(_scatter_ln_body, cap=cap, alpha=alpha),
        grid=(bsz, n_exp),
        in_specs=[
            pl.BlockSpec((1, 1, cap), sel, memory_space=pltpu.SMEM),
            pl.BlockSpec((1, 1, cap), sel, memory_space=pltpu.SMEM),
            pl.BlockSpec((None, None, cap // SUBLANES, SUBLANES, d), lambda b, e: (b, e, 0, 0, 0)),
            pl.BlockSpec(memory_space=pl.ANY),
            pl.BlockSpec((None, 1, d), lambda b, e: (b, 0, 0)), par, par,
        ],
        out_specs=pl.BlockSpec(memory_space=pl.ANY),
        out_shape=jax.ShapeDtypeStruct((bsz, seq, d), F32),
        scratch_shapes=[pltpu.VMEM((seq, d), F32), chunk_pair, chunk_pair,
                        pltpu.SemaphoreType.DMA((2,)), pltpu.SemaphoreType.DMA((2,))],
        compiler_params=_cparams(VMEM_LIMIT),
        name="scatter_ln",
    )(idx.reshape(bsz * n_exp, 1, cap), gate.reshape(bsz * n_exp, 1, cap),
      y.reshape(bsz, n_exp, cap // SUBLANES, SUBLANES, d), xm, g2, lg, lb)


def _expert_dense_body(h_ref, w_ref, wg_ref, wu_ref, wd_ref, o_ref):
    e = pl.program_id(0)

    @pl.when(e == 0)
    def _():
        o_ref[...] = jnp.zeros_like(o_ref)

    w = w_ref[...]
    lane = lax.broadcasted_iota(jnp.int32, w.shape, 1)
    gate = jnp.sum(jnp.where(lane == e, w, 0.0), axis=1, keepdims=True)
    o_ref[...] += _swiglu(h_ref[...], wg_ref, wu_ref, wd_ref) * gate


def _expert_dense(h, w, wg, wu, wd, layer):
    t, d = h.shape
    n_exp, _, ff = wg.shape[1:]
    return pl.pallas_call(
        _expert_dense_body,
        grid=(n_exp,),
        in_specs=[
            pl.BlockSpec((t, d), lambda e: (0, 0)),
            pl.BlockSpec(w.shape, lambda e: (0, 0)),
            pl.BlockSpec((None, None, d, ff), lambda e: (layer, e, 0, 0)),
            pl.BlockSpec((None, None, d, ff), lambda e: (layer, e, 0, 0)),
            pl.BlockSpec((None, None, ff, d), lambda e: (layer, e, 0, 0)),
        ],
        out_specs=pl.BlockSpec((t, d), lambda e: (0, 0)),
        out_shape=jax.ShapeDtypeStruct((t, d), F32),
        compiler_params=_cparams(VMEM_LIMIT),
        name="expert_dense",
    )(h, w, wg, wu, wd)


def _final_ln_body(xm_ref, f_ref, g2_ref, lg_ref, lb_ref, o_ref, *, alpha):
    z = alpha * xm_ref[...] + g2_ref[...] * f_ref[...]
    o_ref[...] = _ln(z) * lg_ref[...] + lb_ref[...]


def _final_ln(xm, ffn, g2, lg, lb, tm, alpha):
    bsz, seq, d = xm.shape
    tok = pl.BlockSpec((None, tm, d), lambda b, i: (b, i, 0))
    par = pl.BlockSpec((1, d), lambda b, i: (0, 0))
    return pl.pallas_call(
        functools.partial(_final_ln_body, alpha=alpha),
        grid=(bsz, seq // tm),
        in_specs=[tok, tok, pl.BlockSpec((None, 1, d), lambda b, i: (b, 0, 0)), par, par],
        out_specs=tok,
        out_shape=jax.ShapeDtypeStruct((bsz, seq, d), F32),
        compiler_params=_cparams(VMEM_LIMIT),
        name="final_ln",
    )(xm, ffn, g2, lg, lb)


def _rope_tables(seq):
    t = np.arange(seq)
    inv = ROPE_THETA ** (-np.arange(0, ROPE_AXIS_DIM, 2, dtype=np.float32) / ROPE_AXIS_DIM)
    inv = jnp.asarray(inv, F32)

    def axis(pos):
        ang = jnp.asarray(pos, F32)[:, None] * inv[None, :]
        c, s = jnp.cos(ang), jnp.sin(ang)
        return jnp.concatenate([c, c], axis=1), jnp.concatenate([-s, s], axis=1)

    cr, sr = axis(t // GRID_W)
    cc, sc = axis(t % GRID_W)
    reps = LANES // HEAD_DIM
    return (jnp.tile(jnp.concatenate([cr, cc], axis=1), (1, reps)),
            jnp.tile(jnp.concatenate([sr, sc], axis=1), (1, reps)))


def _token_tile(seq):
    return 512 if seq % 512 == 0 else seq


def kernel(x, c, ctx, c_ctx, w_mod, b_mod, w_in, conv_w, attn_sink, na_rpb, w_out, ln1_g, ln1_b,
           w_router, w_gate, w_up, w_down, ln2_g, ln2_b):
    bsz, seq, d = x.shape
    lctx = ctx.shape[1]
    depth = w_mod.shape[0]
    alpha = (2 * depth) ** 0.25
    nb = seq // A_BLOCK
    assert seq % A_BLOCK == 0 and nb >= WIN_BLOCKS + 1 and bsz + 1 <= MOD_ROWS
    tm = _token_tile(seq)
    tmc = _token_tile(bsz * lctx)

    cond = jnp.zeros((MOD_ROWS, d), F32).at[:bsz].set(c).at[bsz].set(c_ctx)
    mod = _modulation(cond, w_mod, b_mod)

    cos, sin = _rope_tables(seq)
    cos_c = jnp.ones((bsz * lctx, LANES), F32)
    sin_c = jnp.zeros((bsz * lctx, LANES), F32)

    head_order = [t * A_GROUP + j for j in range(A_GROUP) for t in range(A_KV_HEADS)]

    def regroup_heads(w, axis):
        take = lambda a, b: lax.slice_in_dim(w, a, b, axis=axis)
        parts = [take(h * HEAD_DIM, (h + 1) * HEAD_DIM) for h in head_order]
        return jnp.concatenate(parts + [take(A_WIDTH, w.shape[axis])], axis=axis).astype(BF16)

    cap = CAPACITY_FACTOR * seq // N_EXPERTS
    cap_c = CAPACITY_FACTOR * lctx // N_EXPERTS
    ctx_flat = ctx.reshape(1, bsz * lctx, d)

    for l in range(depth):
        last = l == depth - 1
        m6 = mod[l].reshape(MOD_ROWS, N_MOD, d)
        sh1, sc1, g1, sh2, sc2, g2 = (m6[:bsz, i].reshape(bsz, 1, d) for i in range(N_MOD))
        csh1, csc1, cg1, csh2, csc2, cg2 = (m6[bsz:bsz + 1, i].reshape(1, 1, d) for i in range(N_MOD))
        w_in_l = regroup_heads(w_in[l], 1)
        w_out_l = regroup_heads(w_out[l], 0)
        w_r = jnp.zeros((d, LANES), BF16).at[:, :N_EXPERTS].set(w_router[l].astype(BF16))
        sink = attn_sink[l]
        lg1, lb1 = ln1_g[l].reshape(1, d), ln1_b[l].reshape(1, d)
        lg2, lb2 = ln2_g[l].reshape(1, d), ln2_b[l].reshape(1, d)

        qa, kva, ub, qn, kvn = _inproj(x, sh1, sc1, cos, sin, w_in_l, tm)
        qa_c, kva_c, ub_c, qn_c, kvn_c = (
            a.reshape(bsz, lctx, a.shape[-1])
            for a in _inproj(ctx_flat, csh1, csc1, cos_c, sin_c, w_in_l, tmc))
        bias = _na_bias_tables(na_rpb[l], nb)
        o = _mix_latent(sink, qa, kva, ub, qn, kvn, kva_c, kvn_c, bias, conv_w[l])
        x_mid, hp, aff_t = _outproj(o, x, g1, lg1, lb1, sh2, sc2, w_out_l, w_r, tm, alpha, True)

        if not last:
            o_c = _mix_ctx(sink, qa_c, kva_c, ub_c, qn_c, kvn_c, conv_w[l])
            ctx_mid, h2c, aff_ct = _outproj(
                o_c.reshape(1, bsz * lctx, d), ctx_flat, cg1, lg1, lb1, csh2, csc2, w_out_l, w_r,
                tmc, alpha, False)
            aff_ct = jnp.transpose(aff_ct.reshape(N_EXPERTS, bsz, lctx), (1, 0, 2))
            gate_c = _select_ctx(aff_ct, cap_c)
            gate_c = jnp.transpose(gate_c, (0, 2, 1)).reshape(bsz * lctx, N_EXPERTS)
            ffn_c = _expert_dense(h2c[0], gate_c, w_gate, w_up, w_down, l)
            ctx_flat = _final_ln(ctx_mid, ffn_c[None], cg2, lg2, lb2, tmc, alpha)

        idx, gate = _select_latent(aff_t, cap)
        y = _expert_gather(idx, hp, w_gate, w_up, w_down, l)
        x = _scatter_ln(idx, gate, y, x_mid, g2, lg2, lb2, alpha)
    return x
```

```python
import functools

import numpy as np
import jax
import jax.numpy as jnp
from jax import lax
from jax.experimental import pallas as pl
from jax.experimental.pallas import tpu as pltpu

HEAD_DIM = 64
GRID_W = 64
A_Q_HEADS = 6
A_KV_HEADS = 2
A_GROUP = A_Q_HEADS // A_KV_HEADS
A_WINDOW = 128
A_BLOCK = 128
B_WIDTH = 256
C_HEADS = 6
NA_ROWS = 8
NA_COLS = 16
A_WIDTH = A_Q_HEADS * HEAD_DIM
A_KV_WIDTH = A_KV_HEADS * HEAD_DIM
C_WIDTH = C_HEADS * HEAD_DIM
N_EXPERTS = 16
CAPACITY_FACTOR = 2
ROPE_THETA = 10000.0
ROPE_AXIS_DIM = HEAD_DIM // 2
LN_EPS = 1e-6
N_MOD = 6
NEG_INF = -1e30

LANES = 128
SUBLANES = 8
MXU_COLS = 256
MOD_ROWS = SUBLANES
HALO_ROWS = 16
WIN_BLOCKS = 5
N_BIAS_PATTERNS = 5
VMEM_LIMIT = 56 * 1024 * 1024

BF16 = jnp.bfloat16
F32 = jnp.float32


def _cparams(vmem=None):
    return pltpu.CompilerParams(vmem_limit_bytes=vmem) if vmem else None


def _dot(a, b):
    return jnp.dot(a, b, preferred_element_type=F32)


def _dot_nt(a, b):
    return lax.dot_general(a, b, (((1,), (1,)), ((), ())), preferred_element_type=F32)


def _ln(x):
    mu = jnp.mean(x, axis=-1, keepdims=True)
    xc = x - mu
    var = jnp.mean(xc * xc, axis=-1, keepdims=True)
    return xc * lax.rsqrt(var + LN_EPS)


def _mod_body(c_ref, w_ref, b_ref, o_ref):
    c = c_ref[...]
    h = (c * jax.nn.sigmoid(c)).astype(BF16)
    o_ref[...] = _dot(h, w_ref[...].astype(BF16)) + b_ref[...]


def _modulation(cond, w_mod, b_mod):
    depth, d, n = w_mod.shape
    tn = n // 4
    return pl.pallas_call(
        _mod_body,
        grid=(depth, n // tn),
        in_specs=[
            pl.BlockSpec((MOD_ROWS, d), lambda l, j: (0, 0)),
            pl.BlockSpec((None, d, tn), lambda l, j: (l, 0, j)),
            pl.BlockSpec((None, 1, tn), lambda l, j: (l, 0, j)),
        ],
        out_specs=pl.BlockSpec((None, MOD_ROWS, tn), lambda l, j: (l, 0, j)),
        out_shape=jax.ShapeDtypeStruct((depth, MOD_ROWS, n), F32),
        compiler_params=_cparams(VMEM_LIMIT),
        name="modulation",
    )(cond, w_mod, b_mod.reshape(depth, 1, n))


QK_W = A_WIDTH + HEAD_DIM * A_KV_HEADS
O_VA = QK_W
O_B = O_VA + A_KV_WIDTH
O_QN = O_B + 3 * B_WIDTH
O_KVN = O_QN + C_WIDTH
IN_WIDTH = O_KVN + 2 * C_WIDTH


def _inproj_body(x_ref, sh_ref, sc_ref, cos_ref, sin_ref, w_ref,
                 qa_ref, kva_ref, ub_ref, qn_ref, kvn_ref):
    h = _ln(x_ref[...]) * (1.0 + sc_ref[...]) + sh_ref[...]
    hb = h.astype(BF16)
    scale = HEAD_DIM ** -0.5
    qk = _dot(hb, w_ref[:, 0:QK_W])
    cos = cos_ref[...]
    sin = sin_ref[...]
    lane = lax.broadcasted_iota(jnp.int32, cos.shape, 1)
    first = (lane % ROPE_AXIS_DIM) < (ROPE_AXIS_DIM // 2)
    half = ROPE_AXIS_DIM // 2
    parts = []
    for j in range(QK_W // LANES):
        t = qk[:, j * LANES:(j + 1) * LANES]
        partner = jnp.where(first, pltpu.roll(t, LANES - half, 1), pltpu.roll(t, half, 1))
        parts.append(t * cos + partner * sin)
    qa_ref[...] = (jnp.concatenate(parts[:-1], axis=1) * scale).astype(BF16)
    va = _dot(hb, w_ref[:, O_VA:O_B])
    kva_ref[...] = jnp.concatenate([parts[-1], va], axis=1).astype(BF16)
    g = _dot(hb, w_ref[:, O_B:O_QN])
    bx, bb, bc = g[:, :B_WIDTH], g[:, B_WIDTH:2 * B_WIDTH], g[:, 2 * B_WIDTH:]
    ub_ref[...] = jnp.concatenate([bc * bx, bb], axis=1).astype(BF16)
    qn_ref[...] = (_dot(hb, w_ref[:, O_QN:O_KVN]) * scale).astype(BF16)
    kvn_ref[...] = _dot(hb, w_ref[:, O_KVN:IN_WIDTH]).astype(BF16)


def _inproj(x, sh, sc, cos, sin, w, tm):
    bsz, seq, d = x.shape
    tok = lambda width: pl.BlockSpec((None, tm, width), lambda b, i: (b, i, 0))
    vec = pl.BlockSpec((None, 1, d), lambda b, i: (b, 0, 0))
    tab = pl.BlockSpec((tm, LANES), lambda b, i: (i, 0))
    widths = (A_WIDTH, 2 * A_KV_WIDTH, 2 * B_WIDTH, C_WIDTH, 2 * C_WIDTH)
    return pl.pallas_call(
        _inproj_body,
        grid=(bsz, seq // tm),
        in_specs=[tok(d), vec, vec, tab, tab, pl.BlockSpec((d, IN_WIDTH), lambda b, i: (0, 0))],
        out_specs=[tok(wd) for wd in widths],
        out_shape=[jax.ShapeDtypeStruct((bsz, seq, wd), BF16) for wd in widths],
        compiler_params=_cparams(VMEM_LIMIT),
        name="inproj",
    )(x, sh, sc, cos, sin, w)


def _split_heads(q, lo):
    zero = jnp.zeros_like(q)
    return jnp.concatenate([jnp.where(lo, q, zero), jnp.where(lo, zero, q)], axis=0)


def _with_ones(v_cat):
    return jnp.concatenate([v_cat, jnp.ones_like(v_cat)], axis=1)


def _attend(units, m_rows):
    lo = lax.broadcasted_iota(jnp.int32, (m_rows, LANES), 1) < HEAD_DIM
    scores = lambda unit: _dot_nt(_split_heads(unit[0], lo), unit[1])
    s_next = scores(units[0])
    outs = []
    for i, (_, _, values, probs) in enumerate(units):
        s = s_next
        if i + 1 < len(units):
            s_next = scores(units[i + 1])
        e, extra = probs(s)
        o = _dot(e, values)
        den = o[:, LANES:] if extra is None else o[:, LANES:] + extra
        o = o[:, :LANES] / den
        outs.append(jnp.where(lo, o[:m_rows], o[m_rows:]))
    return outs


def _gqa_probs(sink_ref, mask, tile, m_rows):
    def probs(s):
        es, sinks = [], []
        for half in range(A_KV_HEADS):
            sb = s[half * m_rows:(half + 1) * m_rows]
            if mask is not None:
                sb = jnp.where(mask, sb, NEG_INF)
            sk = sink_ref[half * A_GROUP + tile]
            m = jnp.maximum(jnp.max(sb, axis=1, keepdims=True), sk)
            es.append(jnp.exp((sb - m).astype(BF16)))
            sinks.append(jnp.exp(sk - m))
        return jnp.concatenate(es, axis=0), jnp.concatenate(sinks, axis=0)
    return probs


def _mha_probs(bias_pair, n_biased):
    def probs(s):
        if bias_pair is None:
            return jnp.exp((s - jnp.max(s, axis=1, keepdims=True)).astype(BF16)), None
        s_w = s[:, :n_biased] + bias_pair
        s_c = s[:, n_biased:]
        m = jnp.maximum(jnp.max(s_w, axis=1, keepdims=True), jnp.max(s_c, axis=1, keepdims=True))
        return jnp.concatenate([jnp.exp((s_w - m).astype(BF16)), jnp.exp((s_c - m).astype(BF16))], axis=1), None
    return probs


def _short_conv(ub, prev_row, next_row, w_ref, m_rows):
    u = ub[:, :B_WIDTH].astype(F32)
    bb = ub[:, B_WIDTH:].astype(F32)
    row = lax.broadcasted_iota(jnp.int32, u.shape, 0)
    u_m1 = jnp.where(row == 0, prev_row, pltpu.roll(u, 1, 0))
    u_p1 = jnp.where(row == m_rows - 1, next_row, pltpu.roll(u, m_rows - 1, 0))
    y = u_m1 * w_ref[0:1, :] + u * w_ref[1:2, :] + u_p1 * w_ref[2:3, :]
    return bb * y


def _mix_latent_body(sink_ref, qa_ref, kp_ref, ko_ref, kn_ref, ubp_ref, ubo_ref, ubn_ref, qn_ref,
                     w0_ref, w1_ref, w2_ref, w3_ref, w4_ref, kvac_ref, kvnc_ref, bias_ref, cw_ref,
                     o_ref, *, seq):
    n = pl.program_id(1)
    nb = pl.num_programs(1)
    m_rows = A_BLOCK
    k_cat = jnp.concatenate([kp_ref[:, :A_KV_WIDTH], ko_ref[:, :A_KV_WIDTH], kn_ref[:, :A_KV_WIDTH],
                             kvac_ref[:, :A_KV_WIDTH]], axis=0)
    v_cat = jnp.concatenate([kp_ref[:, A_KV_WIDTH:], ko_ref[:, A_KV_WIDTH:], kn_ref[:, A_KV_WIDTH:],
                             kvac_ref[:, A_KV_WIDTH:]], axis=0)
    n_keys = k_cat.shape[0]
    qpos = n * A_BLOCK + lax.broadcasted_iota(jnp.int32, (m_rows, n_keys), 0)
    col = lax.broadcasted_iota(jnp.int32, (m_rows, n_keys), 1)
    kpos = (n - 1) * A_BLOCK + col
    mask = (col >= 3 * A_BLOCK) | ((jnp.abs(qpos - kpos) <= A_WINDOW) & (kpos >= 0) & (kpos < seq))
    v_ones = _with_ones(v_cat)
    units = [(qa_ref[:, j * LANES:(j + 1) * LANES], k_cat, v_ones, _gqa_probs(sink_ref, mask, j, m_rows))
             for j in range(A_WIDTH // LANES)]
    wins = (w0_ref, w1_ref, w2_ref, w3_ref, w4_ref)
    n_win = WIN_BLOCKS * A_BLOCK
    for j in range(C_WIDTH // LANES):
        ks = slice(j * LANES, (j + 1) * LANES)
        vs = slice(C_WIDTH + j * LANES, C_WIDTH + (j + 1) * LANES)
        kc = jnp.concatenate([w[:, ks] for w in wins] + [kvnc_ref[:, ks]], axis=0)
        vc = jnp.concatenate([w[:, vs] for w in wins] + [kvnc_ref[:, vs]], axis=0)
        bias_pair = jnp.concatenate([bias_ref[2 * j], bias_ref[2 * j + 1]], axis=0)
        units.append((qn_ref[:, ks], kc, _with_ones(vc), _mha_probs(bias_pair, n_win)))
    outs = _attend(units, m_rows)
    prev_row = jnp.where(n > 0, ubp_ref[HALO_ROWS - 1:HALO_ROWS, :B_WIDTH].astype(F32), 0.0)
    next_row = jnp.where(n < nb - 1, ubn_ref[0:1, :B_WIDTH].astype(F32), 0.0)
    o_b = _short_conv(ubo_ref[...], prev_row, next_row, cw_ref, m_rows)
    n_a = A_WIDTH // LANES
    o_ref[...] = jnp.concatenate(outs[:n_a] + [o_b] + outs[n_a:], axis=1).astype(BF16)


def _mix_latent(sink, qa, kva, ub, qn, kvn, kva_c, kvn_c, bias, conv_w):
    bsz, seq, _ = qa.shape
    nb = seq // A_BLOCK
    lctx = kva_c.shape[1]
    blk = A_BLOCK
    hb = blk // HALO_ROWS
    n_halo = seq // HALO_ROWS
    tok = lambda width, fn: pl.BlockSpec((None, blk, width), fn)
    own = lambda b, n: (b, n, 0)
    win = lambda i: (lambda b, n: (b, jnp.clip(n - 2, 0, nb - WIN_BLOCKS) + i, 0))
    pattern = lambda b, n: (jnp.minimum(n, 2) + jnp.maximum(n - (nb - 3), 0), 0, 0, 0)
    in_specs = [
        pl.BlockSpec(memory_space=pltpu.SMEM),
        tok(A_WIDTH, own),
        tok(2 * A_KV_WIDTH, lambda b, n: (b, jnp.maximum(n - 1, 0), 0)),
        tok(2 * A_KV_WIDTH, own),
        tok(2 * A_KV_WIDTH, lambda b, n: (b, jnp.minimum(n + 1, nb - 1), 0)),
        pl.BlockSpec((None, HALO_ROWS, 2 * B_WIDTH), lambda b, n: (b, jnp.maximum(n * hb - 1, 0), 0)),
        tok(2 * B_WIDTH, own),
        pl.BlockSpec((None, HALO_ROWS, 2 * B_WIDTH), lambda b, n: (b, jnp.minimum((n + 1) * hb, n_halo - 1), 0)),
        tok(C_WIDTH, own),
    ] + [tok(2 * C_WIDTH, win(i)) for i in range(WIN_BLOCKS)] + [
        pl.BlockSpec((None, lctx, 2 * A_KV_WIDTH), lambda b, n: (b, 0, 0)),
        pl.BlockSpec((None, lctx, 2 * C_WIDTH), lambda b, n: (b, 0, 0)),
        pl.BlockSpec((None, C_HEADS, blk, WIN_BLOCKS * blk), pattern),
        pl.BlockSpec(conv_w.shape, lambda b, n: (0, 0)),
    ]
    return pl.pallas_call(
        functools.partial(_mix_latent_body, seq=seq),
        grid=(bsz, nb),
        in_specs=in_specs,
        out_specs=tok(A_WIDTH + B_WIDTH + C_WIDTH, own),
        out_shape=jax.ShapeDtypeStruct((bsz, seq, A_WIDTH + B_WIDTH + C_WIDTH), BF16),
        compiler_params=_cparams(VMEM_LIMIT),
        name="mix_latent",
    )(sink, qa, kva, kva, kva, ub, ub, ub, qn, kvn, kvn, kvn, kvn, kvn, kva_c, kvn_c, bias, conv_w)


def _mix_ctx_body(sink_ref, qa_ref, kva_ref, ub_ref, qn_ref, kvn_ref, cw_ref, o_ref):
    m_rows = qa_ref.shape[0]
    v_ones = _with_ones(kva_ref[:, A_KV_WIDTH:])
    units = [(qa_ref[:, j * LANES:(j + 1) * LANES], kva_ref[:, :A_KV_WIDTH], v_ones,
              _gqa_probs(sink_ref, None, j, m_rows)) for j in range(A_WIDTH // LANES)]
    for j in range(C_WIDTH // LANES):
        ks = slice(j * LANES, (j + 1) * LANES)
        vs = slice(C_WIDTH + j * LANES, C_WIDTH + (j + 1) * LANES)
        units.append((qn_ref[:, ks], kvn_ref[:, ks], _with_ones(kvn_ref[:, vs]), _mha_probs(None, 0)))
    outs = _attend(units, m_rows)
    o_b = _short_conv(ub_ref[...], 0.0, 0.0, cw_ref, m_rows)
    n_a = A_WIDTH // LANES
    o_ref[...] = jnp.concatenate(outs[:n_a] + [o_b] + outs[n_a:], axis=1).astype(BF16)


def _mix_ctx(sink, qa, kva, ub, qn, kvn, conv_w):
    bsz, lctx, _ = qa.shape
    full = lambda width: pl.BlockSpec((None, lctx, width), lambda b: (b, 0, 0))
    width = A_WIDTH + B_WIDTH + C_WIDTH
    return pl.pallas_call(
        _mix_ctx_body,
        grid=(bsz,),
        in_specs=[pl.BlockSpec(memory_space=pltpu.SMEM), full(A_WIDTH), full(2 * A_KV_WIDTH),
                  full(2 * B_WIDTH), full(C_WIDTH), full(2 * C_WIDTH),
                  pl.BlockSpec(conv_w.shape, lambda b: (0, 0))],
        out_specs=full(width),
        out_shape=jax.ShapeDtypeStruct((bsz, lctx, width), BF16),
        compiler_params=_cparams(VMEM_LIMIT),
        name="mix_ctx",
    )(sink, qa, kva, ub, qn, kvn, conv_w)


def _na_bias_tables(rpb, nb):
    per_blk = A_BLOCK // GRID_W
    rows = nb * per_blk
    kh = min(NA_ROWS, rows)
    qc = np.arange(GRID_W)[:, None]
    kc = np.arange(GRID_W)[None, :]
    coff = np.clip(kc - qc, -(NA_COLS - 1), NA_COLS - 1) + (NA_COLS - 1)
    cs = np.clip(qc - NA_COLS // 2, 0, GRID_W - NA_COLS)
    col_ok = (kc >= cs) & (kc < cs + NA_COLS)
    onehot = (coff[None] == np.arange(2 * NA_COLS - 1)[:, None, None]).astype(np.float32)
    tiles = jnp.einsum('hrc,cqk->hrqk', rpb, onehot, precision=lax.Precision.HIGHEST)
    tiles = jnp.where(col_ok, tiles, NEG_INF)
    masked = jnp.full((rpb.shape[0], GRID_W, GRID_W), NEG_INF, F32)
    patterns = []
    for blk in (0, 1, 2, nb - 2, nb - 1):
        wb = min(max(blk - 2, 0), nb - WIN_BLOCKS)
        q_rows = []
        for qi in range(per_blk):
            qr = per_blk * blk + qi
            rs = min(max(qr - kh // 2, 0), rows - kh)
            k_tiles = []
            for kj in range(WIN_BLOCKS * per_blk):
                kr = per_blk * wb + kj
                k_tiles.append(tiles[:, kr - qr + NA_ROWS - 1] if rs <= kr < rs + kh else masked)
            q_rows.append(jnp.concatenate(k_tiles, axis=-1))
        patterns.append(jnp.concatenate(q_rows, axis=-2))
    return jnp.stack(patterns)


def _router_affinity(h2b, wr_ref):
    logits = _dot(h2b, wr_ref[...])
    lane = lax.broadcasted_iota(jnp.int32, logits.shape, 1)
    logits = jnp.where(lane < N_EXPERTS, logits, NEG_INF)
    e = jnp.exp(logits - jnp.max(logits, axis=1, keepdims=True))
    return e / jnp.sum(e, axis=1, keepdims=True)


def _outproj_core(o_ref, x_ref, g1_ref, lg_ref, lb_ref, sh_ref, sc_ref, wo_ref, wr_ref, alpha):
    mix = _dot(o_ref[...], wo_ref[...])
    z = alpha * x_ref[...] + g1_ref[...] * mix
    x_mid = _ln(z) * lg_ref[...] + lb_ref[...]
    h2 = _ln(x_mid) * (1.0 + sc_ref[...]) + sh_ref[...]
    return x_mid, h2, _router_affinity(h2.astype(BF16), wr_ref)


def _outproj_body(o_ref, x_ref, g1_ref, lg_ref, lb_ref, sh_ref, sc_ref, wo_ref, wr_ref,
                  xm_ref, h2_ref, afft_ref, *, alpha, packed):
    x_mid, h2, aff = _outproj_core(o_ref, x_ref, g1_ref, lg_ref, lb_ref, sh_ref, sc_ref, wo_ref, wr_ref, alpha)
    xm_ref[...] = x_mid
    if packed:
        half = h2.shape[1] // 2
        h2_ref[...] = pltpu.pack_elementwise([h2[:, :half], h2[:, half:]], packed_dtype=BF16)
    else:
        h2_ref[...] = h2.astype(BF16)
    afft_ref[...] = aff.T[:N_EXPERTS, :]


def _outproj(o, x, g1, lg, lb, sh, sc, wo, wr, tm, alpha, latent):
    bsz, seq, d = x.shape
    tok = lambda width: pl.BlockSpec((None, tm, width), lambda b, i: (b, i, 0))
    vec = pl.BlockSpec((None, 1, d), lambda b, i: (b, 0, 0))
    par = pl.BlockSpec((1, d), lambda b, i: (0, 0))
    afft_spec = pl.BlockSpec((None, N_EXPERTS, tm), lambda b, i: (b, 0, i))
    afft_shape = jax.ShapeDtypeStruct((bsz, N_EXPERTS, seq), F32)
    h2_width, h2_dtype = (d // 2, jnp.uint32) if latent else (d, BF16)
    out_specs = [tok(d), tok(h2_width), afft_spec]
    out_shape = [jax.ShapeDtypeStruct((bsz, seq, d), F32), jax.ShapeDtypeStruct((bsz, seq, h2_width), h2_dtype),
                 afft_shape]
    return pl.pallas_call(
        functools.partial(_outproj_body, alpha=alpha, packed=latent),
        grid=(bsz, seq // tm),
        in_specs=[tok(d), tok(d), vec, par, par, vec, vec,
                  pl.BlockSpec(wo.shape, lambda b, i: (0, 0)), pl.BlockSpec(wr.shape, lambda b, i: (0, 0))],
        out_specs=out_specs,
        out_shape=out_shape,
        compiler_params=_cparams(VMEM_LIMIT),
        name="outproj_latent" if latent else "outproj_ctx",
    )(o, x, g1, lg, lb, sh, sc, wo, wr)


def _kth_largest_bucket(aff, cap, axes):
    shape = list(aff.shape)
    for a in axes:
        shape[a] = 1
    thr = jnp.zeros(shape, jnp.int32)
    for bit in range(30, -1, -1):
        cand = thr | jnp.int32(1 << bit)
        cnt = _count(aff >= lax.bitcast_convert_type(cand, F32), axes)
        thr = jnp.where(cnt >= cap, cand, thr)
    return lax.bitcast_convert_type(thr, F32), lax.bitcast_convert_type(thr + 1, F32)


def _count(mask, axes):
    c = jnp.where(mask, 1.0, 0.0)
    for a in sorted(axes):
        c = jnp.sum(c, axis=a, keepdims=True)
    return c


def _select_latent_body(a_ref, u_ref, lt_ref, lb_ref, idx_ref, gate_ref, *, cap, nc):
    aff = a_ref[...]
    rows = aff.shape[0]
    aff3 = aff.reshape(N_EXPERTS, nc, LANES)
    t_lo, t_hi = _kth_largest_bucket(aff3, cap, (1, 2))
    gt = aff3 >= t_hi
    eq = (aff3 >= t_lo) & (aff3 < t_hi)
    need = cap - _count(gt, (1, 2))
    eq_f = jnp.where(eq, 1.0, 0.0).reshape(rows, LANES)
    eq_b = eq_f.astype(BF16)
    rank = (_dot(eq_b, u_ref[...]) - eq_f
            + jnp.sum(_dot(lb_ref[...], eq_b), axis=1, keepdims=True))
    sel3 = gt | (eq & (rank.reshape(N_EXPERTS, nc, LANES) < need))
    sel_f = jnp.where(sel3, 1.0, 0.0).reshape(rows, LANES)
    sel_b = sel_f.astype(BF16)
    cnt = jnp.sum(sel_f, axis=1, keepdims=True)
    g_incl = jnp.sum(_dot(lb_ref[...], sel_b), axis=1, keepdims=True) + cnt
    r_row = lax.broadcasted_iota(jnp.int32, (1, cap), 1).astype(F32)
    c_col = lax.broadcasted_iota(jnp.int32, (nc, 1), 0).astype(F32)
    l_col = lax.broadcasted_iota(jnp.int32, (LANES, 1), 0).astype(F32)
    for e in range(N_EXPERTS):
        sl = slice(e * nc, (e + 1) * nc)
        before = g_incl[sl] <= r_row
        chunk = jnp.sum(jnp.where(before, 1.0, 0.0), axis=0, keepdims=True)
        base = jnp.sum(jnp.where(before, cnt[sl], 0.0), axis=0, keepdims=True)
        onehot = jnp.where(c_col == chunk, 1.0, 0.0)
        cs_t = _dot_nt(lt_ref[...], sel_b[sl]).astype(BF16)
        cs_of_r = _dot(cs_t, onehot.astype(BF16))
        local = jnp.sum(jnp.where(cs_of_r <= r_row - base, 1.0, 0.0), axis=0, keepdims=True)
        idx_ref[e:e + 1, :] = (chunk * LANES + local).astype(jnp.int32)
        aff_of_r = jnp.dot(aff[sl].T, onehot, precision=lax.Precision.HIGHEST, preferred_element_type=F32)
        gate_ref[e:e + 1, :] = jnp.sum(jnp.where(l_col == local, aff_of_r, 0.0), axis=0, keepdims=True)


def _select_latent(aff_t, cap):
    bsz, n_exp, seq = aff_t.shape
    nc = seq // LANES
    rows = n_exp * nc
    tri = np.triu(np.ones((LANES, LANES), np.float32))
    r = np.arange(rows)
    blockdiag = ((r[:, None] // nc == r[None, :] // nc) & (r[None, :] < r[:, None])).astype(np.float32)
    const = lambda shape: pl.BlockSpec(shape, lambda b: (0, 0))
    return pl.pallas_call(
        functools.partial(_select_latent_body, cap=cap, nc=nc),
        grid=(bsz,),
        in_specs=[pl.BlockSpec((None, rows, LANES), lambda b: (b, 0, 0)),
                  const((LANES, LANES)), const((LANES, LANES)), const((rows, rows))],
        out_specs=[pl.BlockSpec((None, n_exp, cap), lambda b: (b, 0, 0))] * 2,
        out_shape=[jax.ShapeDtypeStruct((bsz, n_exp, cap), jnp.int32),
                   jax.ShapeDtypeStruct((bsz, n_exp, cap), F32)],
        compiler_params=_cparams(VMEM_LIMIT),
        name="select_latent",
    )(aff_t.reshape(bsz, rows, LANES), jnp.asarray(tri, BF16), jnp.asarray(tri.T, BF16),
      jnp.asarray(blockdiag, BF16))


def _select_ctx_body(a_ref, u_ref, w_ref, *, cap):
    aff = a_ref[...]
    t_lo, t_hi = _kth_largest_bucket(aff, cap, (1,))
    gt = aff >= t_hi
    eq = (aff >= t_lo) & (aff < t_hi)
    need = cap - _count(gt, (1,))
    eq_f = jnp.where(eq, 1.0, 0.0)
    rank = _dot(eq_f.astype(BF16), u_ref[...]) - eq_f
    w_ref[...] = jnp.where(gt | (eq & (rank < need)), aff, 0.0)


def _select_ctx(aff_t, cap):
    bsz, n_exp, lctx = aff_t.shape
    tri = np.triu(np.ones((lctx, lctx), np.float32))
    blk = pl.BlockSpec((None, n_exp, lctx), lambda b: (b, 0, 0))
    return pl.pallas_call(
        functools.partial(_select_ctx_body, cap=cap),
        grid=(bsz,),
        in_specs=[blk, pl.BlockSpec((lctx, lctx), lambda b: (0, 0))],
        out_specs=blk,
        out_shape=jax.ShapeDtypeStruct((bsz, n_exp, lctx), F32),
        name="select_ctx",
    )(aff_t, jnp.asarray(tri, BF16))


def _swiglu(xe, wg_ref, wu_ref, wd_ref):
    a = _dot(xe, wg_ref[...].astype(BF16))
    u = _dot(xe, wu_ref[...].astype(BF16))
    act = (a * jax.nn.sigmoid(a) * u).astype(BF16)
    return _dot(act, wd_ref[...].astype(BF16))


def _expert_gather_body(idx_ref, nxt_ref, hp_ref, wg_ref, wu_ref, wd_ref, y_ref, xa_ref, xb_ref, *, cap):
    e = pl.program_id(1)
    groups = cap // SUBLANES

    def gather_group(rows_ref, dst_ref, g):
        for k in range(SUBLANES):
            dst_ref[g, k:k + 1, :] = hp_ref[pl.ds(rows_ref[0, 0, g * SUBLANES + k], 1), :]

    @pl.when(e == 0)
    def _():
        def body(g, carry):
            gather_group(idx_ref, xa_ref, g)
            return carry
        lax.fori_loop(0, groups, body, 0)

    def ffn_and_prefetch(cur_ref, nxt_buf_ref):
        packed = cur_ref[...].reshape(cap, cur_ref.shape[-1])
        halves = [pltpu.unpack_elementwise(packed, index=i, packed_dtype=BF16, unpacked_dtype=F32).astype(BF16)
                  for i in range(2)]
        xe = jnp.concatenate(halves, axis=1)
        ff, d = wd_ref.shape
        n_pieces = 2 * (ff // MXU_COLS) + d // MXU_COLS
        per_piece = groups // n_pieces
        done = [0]

        def prefetch_slice(last=False):
            stop = groups if last else done[0] + per_piece
            for g in range(done[0], stop):
                gather_group(nxt_ref, nxt_buf_ref, g)
            done[0] = stop

        acts = []
        for j in range(ff // MXU_COLS):
            cols = slice(j * MXU_COLS, (j + 1) * MXU_COLS)
            a = _dot(xe, wg_ref[:, cols].astype(BF16))
            prefetch_slice()
            u = _dot(xe, wu_ref[:, cols].astype(BF16))
            prefetch_slice()
            acts.append((a * jax.nn.sigmoid(a) * u).astype(BF16))
        act = jnp.concatenate(acts, axis=1)
        for j in range(d // MXU_COLS):
            cols = slice(j * MXU_COLS, (j + 1) * MXU_COLS)
            y_ref[:, cols] = _dot(act, wd_ref[:, cols].astype(BF16))
            prefetch_slice(last=j == d // MXU_COLS - 1)

    @pl.when(e % 2 == 0)
    def _():
        ffn_and_prefetch(xa_ref, xb_ref)

    @pl.when(e % 2 == 1)
    def _():
        ffn_and_prefetch(xb_ref, xa_ref)


def _expert_gather(idx, hp, wg, wu, wd, layer):
    bsz, n_exp, cap = idx.shape
    seq, width = hp.shape[1:]
    d, ff = wg.shape[2:]
    assert n_exp % 2 == 0
    rows = pltpu.VMEM((cap // SUBLANES, SUBLANES, width), jnp.uint32)
    return pl.pallas_call(
        functools.partial(_expert_gather_body, cap=cap),
        grid=(bsz, n_exp),
        in_specs=[
            pl.BlockSpec((1, 1, cap), lambda b, e: (b * n_exp + e, 0, 0), memory_space=pltpu.SMEM),
            pl.BlockSpec((1, 1, cap), lambda b, e: (b * n_exp + jnp.minimum(e + 1, n_exp - 1), 0, 0),
                         memory_space=pltpu.SMEM),
            pl.BlockSpec((None, seq, width), lambda b, e: (b, 0, 0), pipeline_mode=pl.Buffered(1)),
            pl.BlockSpec((None, None, d, ff), lambda b, e: (layer, e, 0, 0)),
            pl.BlockSpec((None, None, d, ff), lambda b, e: (layer, e, 0, 0)),
            pl.BlockSpec((None, None, ff, d), lambda b, e: (layer, e, 0, 0)),
        ],
        out_specs=pl.BlockSpec((None, None, cap, d), lambda b, e: (b, e, 0, 0)),
        out_shape=jax.ShapeDtypeStruct((bsz, n_exp, cap, d), F32),
        scratch_shapes=[rows, rows],
        compiler_params=_cparams(VMEM_LIMIT),
        name="expert_gather",
    )(idx.reshape(bsz * n_exp, 1, cap), idx.reshape(bsz * n_exp, 1, cap), hp, wg, wu, wd)


LN_CHUNK = 512


def _scatter_ln_body(idx_ref, gate_ref, y_ref, xm_hbm, g2_ref, lg_ref, lb_ref, out_hbm, o_ref, xbuf, obuf,
                     sem_in, sem_out, *, cap, alpha):
    b = pl.program_id(0)
    e = pl.program_id(1)
    last = pl.num_programs(1) - 1
    n_chunks = o_ref.shape[0] // LN_CHUNK

    def xm_copy(c, slot):
        return pltpu.make_async_copy(xm_hbm.at[b, pl.ds(c * LN_CHUNK, LN_CHUNK)], xbuf.at[slot], sem_in.at[slot])

    def out_copy(c, slot):
        return pltpu.make_async_copy(obuf.at[slot], out_hbm.at[b, pl.ds(c * LN_CHUNK, LN_CHUNK)], sem_out.at[slot])

    @pl.when(e == 0)
    def _():
        o_ref[...] = jnp.zeros_like(o_ref)

    @pl.when(e == last)
    def _():
        xm_copy(0, 0).start()

    def add_group(g, carry):
        r0 = g * SUBLANES
        ts = [idx_ref[0, 0, r0 + k] for k in range(SUBLANES)]
        acc = [o_ref[pl.ds(ts[k], 1), :] + gate_ref[0, 0, r0 + k] * y_ref[g, k:k + 1, :]
               for k in range(SUBLANES)]
        for k in range(SUBLANES):
            o_ref[pl.ds(ts[k], 1), :] = acc[k]
        return carry

    lax.fori_loop(0, cap // SUBLANES, add_group, 0)

    @pl.when(e == last)
    def _():
        def ln_chunk(c, carry):
            slot = c % 2
            xm_copy(c, slot).wait()

            @pl.when(c + 1 < n_chunks)
            def _():
                xm_copy(c + 1, 1 - slot).start()

            @pl.when(c >= 2)
            def _():
                out_copy(c - 2, slot).wait()

            rows = pl.ds(pl.multiple_of(c * LN_CHUNK, LN_CHUNK), LN_CHUNK)
            z = alpha * xbuf[slot] + g2_ref[...] * o_ref[rows, :]
            obuf[slot] = _ln(z) * lg_ref[...] + lb_ref[...]
            out_copy(c, slot).start()
            return carry

        lax.fori_loop(0, n_chunks, ln_chunk, 0)
        for c in (n_chunks - 2, n_chunks - 1):
            out_copy(c, c % 2).wait()


def _scatter_ln(idx, gate, y, xm, g2, lg, lb, alpha):
    bsz, n_exp, cap, d = y.shape
    seq = xm.shape[1]
    assert seq % LN_CHUNK == 0 and seq // LN_CHUNK >= 2
    sel = lambda b, e: (b * n_exp + e, 0, 0)
    par = pl.BlockSpec((1, d), lambda b, e: (0, 0))
    chunk_pair = pltpu.VMEM((2, LN_CHUNK, d), F32)
    return pl.pallas_call(
        functools.partial(_scatter_ln_body, cap=cap, alpha=alpha),
        grid=(bsz, n_exp),
        in_specs=[
            pl.BlockSpec((1, 1, cap), sel, memory_space=pltpu.SMEM),
            pl.BlockSpec((1, 1, cap), sel, memory_space=pltpu.SMEM),
            pl.BlockSpec((None, None, cap // SUBLANES, SUBLANES, d), lambda b, e: (b, e, 0, 0, 0)),
            pl.BlockSpec(memory_space=pl.ANY),
            pl.BlockSpec((None, 1, d), lambda b, e: (b, 0, 0)), par, par,
        ],
        out_specs=pl.BlockSpec(memory_space=pl.ANY),
        out_shape=jax.ShapeDtypeStruct((bsz, seq, d), F32),
        scratch_shapes=[pltpu.VMEM((seq, d), F32), chunk_pair, chunk_pair,
                        pltpu.SemaphoreType.DMA((2,)), pltpu.SemaphoreType.DMA((2,))],
        compiler_params=_cparams(VMEM_LIMIT),
        name="scatter_ln",
    )(idx.reshape(bsz * n_exp, 1, cap), gate.reshape(bsz * n_exp, 1, cap),
      y.reshape(bsz, n_exp, cap // SUBLANES, SUBLANES, d), xm, g2, lg, lb)


def _expert_dense_body(h_ref, w_ref, wg_ref, wu_ref, wd_ref, o_ref):
    e = pl.program_id(0)

    @pl.when(e == 0)
    def _():
        o_ref[...] = jnp.zeros_like(o_ref)

    w = w_ref[...]
    lane = lax.broadcasted_iota(jnp.int32, w.shape, 1)
    gate = jnp.sum(jnp.where(lane == e, w, 0.0), axis=1, keepdims=True)
    o_ref[...] += _swiglu(h_ref[...], wg_ref, wu_ref, wd_ref) * gate


def _expert_dense(h, w, wg, wu, wd, layer):
    t, d = h.shape
    n_exp, _, ff = wg.shape[1:]
    return pl.pallas_call(
        _expert_dense_body,
        grid=(n_exp,),
        in_specs=[
            pl.BlockSpec((t, d), lambda e: (0, 0)),
            pl.BlockSpec(w.shape, lambda e: (0, 0)),
            pl.BlockSpec((None, None, d, ff), lambda e: (layer, e, 0, 0)),
            pl.BlockSpec((None, None, d, ff), lambda e: (layer, e, 0, 0)),
            pl.BlockSpec((None, None, ff, d), lambda e: (layer, e, 0, 0)),
        ],
        out_specs=pl.BlockSpec((t, d), lambda e: (0, 0)),
        out_shape=jax.ShapeDtypeStruct((t, d), F32),
        compiler_params=_cparams(VMEM_LIMIT),
        name="expert_dense",
    )(h, w, wg, wu, wd)


def _final_ln_body(xm_ref, f_ref, g2_ref, lg_ref, lb_ref, o_ref, *, alpha):
    z = alpha * xm_ref[...] + g2_ref[...] * f_ref[...]
    o_ref[...] = _ln(z) * lg_ref[...] + lb_ref[...]


def _final_ln(xm, ffn, g2, lg, lb, tm, alpha):
    bsz, seq, d = xm.shape
    tok = pl.BlockSpec((None, tm, d), lambda b, i: (b, i, 0))
    par = pl.BlockSpec((1, d), lambda b, i: (0, 0))
    return pl.pallas_call(
        functools.partial(_final_ln_body, alpha=alpha),
        grid=(bsz, seq // tm),
        in_specs=[tok, tok, pl.BlockSpec((None, 1, d), lambda b, i: (b, 0, 0)), par, par],
        out_specs=tok,
        out_shape=jax.ShapeDtypeStruct((bsz, seq, d), F32),
        compiler_params=_cparams(VMEM_LIMIT),
        name="final_ln",
    )(xm, ffn, g2, lg, lb)


def _rope_tables(seq):
    t = np.arange(seq)
    inv = ROPE_THETA ** (-np.arange(0, ROPE_AXIS_DIM, 2, dtype=np.float32) / ROPE_AXIS_DIM)
    inv = jnp.asarray(inv, F32)

    def axis(pos):
        ang = jnp.asarray(pos, F32)[:, None] * inv[None, :]
        c, s = jnp.cos(ang), jnp.sin(ang)
        return jnp.concatenate([c, c], axis=1), jnp.concatenate([-s, s], axis=1)

    cr, sr = axis(t // GRID_W)
    cc, sc = axis(t % GRID_W)
    reps = LANES // HEAD_DIM
    return (jnp.tile(jnp.concatenate([cr, cc], axis=1), (1, reps)),
            jnp.tile(jnp.concatenate([sr, sc], axis=1), (1, reps)))


def _token_tile(seq):
    return 1024 if seq % 1024 == 0 else seq


def kernel(x, c, ctx, c_ctx, w_mod, b_mod, w_in, conv_w, attn_sink, na_rpb, w_out, ln1_g, ln1_b,
           w_router, w_gate, w_up, w_down, ln2_g, ln2_b):
    bsz, seq, d = x.shape
    lctx = ctx.shape[1]
    depth = w_mod.shape[0]
    alpha = (2 * depth) ** 0.25
    nb = seq // A_BLOCK
    assert seq % A_BLOCK == 0 and nb >= WIN_BLOCKS + 1 and bsz + 1 <= MOD_ROWS
    tm = _token_tile(seq)
    tmc = _token_tile(bsz * lctx)

    cond = jnp.zeros((MOD_ROWS, d), F32).at[:bsz].set(c).at[bsz].set(c_ctx)
    mod = _modulation(cond, w_mod, b_mod)

    cos, sin = _rope_tables(seq)
    cos_c = jnp.ones((bsz * lctx, LANES), F32)
    sin_c = jnp.zeros((bsz * lctx, LANES), F32)

    head_order = [t * A_GROUP + j for j in range(A_GROUP) for t in range(A_KV_HEADS)]

    def regroup_heads(w, axis):
        take = lambda a, b: lax.slice_in_dim(w, a, b, axis=axis)
        parts = [take(h * HEAD_DIM, (h + 1) * HEAD_DIM) for h in head_order]
        return jnp.concatenate(parts + [take(A_WIDTH, w.shape[axis])], axis=axis).astype(BF16)

    cap = CAPACITY_FACTOR * seq // N_EXPERTS
    cap_c = CAPACITY_FACTOR * lctx // N_EXPERTS
    ctx_flat = ctx.reshape(1, bsz * lctx, d)

    for l in range(depth):
        last = l == depth - 1
        m6 = mod[l].reshape(MOD_ROWS, N_MOD, d)
        sh1, sc1, g1, sh2, sc2, g2 = (m6[:bsz, i].reshape(bsz, 1, d) for i in range(N_MOD))
        csh1, csc1, cg1, csh2, csc2, cg2 = (m6[bsz:bsz + 1, i].reshape(1, 1, d) for i in range(N_MOD))
        w_in_l = regroup_heads(w_in[l], 1)
        w_out_l = regroup_heads(w_out[l], 0)
        w_r = jnp.zeros((d, LANES), BF16).at[:, :N_EXPERTS].set(w_router[l].astype(BF16))
        sink = attn_sink[l]
        lg1, lb1 = ln1_g[l].reshape(1, d), ln1_b[l].reshape(1, d)
        lg2, lb2 = ln2_g[l].reshape(1, d), ln2_b[l].reshape(1, d)

        qa, kva, ub, qn, kvn = _inproj(x, sh1, sc1, cos, sin, w_in_l, tm)
        qa_c, kva_c, ub_c, qn_c, kvn_c = (
            a.reshape(bsz, lctx, a.shape[-1])
            for a in _inproj(ctx_flat, csh1, csc1, cos_c, sin_c, w_in_l, tmc))
        bias = _na_bias_tables(na_rpb[l], nb)
        o = _mix_latent(sink, qa, kva, ub, qn, kvn, kva_c, kvn_c, bias, conv_w[l])
        x_mid, hp, aff_t = _outproj(o, x, g1, lg1, lb1, sh2, sc2, w_out_l, w_r, tm, alpha, True)

        if not last:
            o_c = _mix_ctx(sink, qa_c, kva_c, ub_c, qn_c, kvn_c, conv_w[l])
            ctx_mid, h2c, aff_ct = _outproj(
                o_c.reshape(1, bsz * lctx, d), ctx_flat, cg1, lg1, lb1, csh2, csc2, w_out_l, w_r,
                tmc, alpha, False)
            aff_ct = jnp.transpose(aff_ct.reshape(N_EXPERTS, bsz, lctx), (1, 0, 2))
            gate_c = _select_ctx(aff_ct, cap_c)
            gate_c = jnp.transpose(gate_c, (0, 2, 1)).reshape(bsz * lctx, N_EXPERTS)
            ffn_c = _expert_dense(h2c[0], gate_c, w_gate, w_up, w_down, l)
            ctx_flat = _final_ln(ctx_mid, ffn_c[None], cg2, lg2, lb2, tmc, alpha)

        idx, gate = _select_latent(aff_t, cap)
        y = _expert_gather(idx, hp, w_gate, w_up, w_down, l)
        x = _scatter_ln(idx, gate, y, x_mid, g2, lg2, lb2, alpha)
    return x
```
